```python
import math
import jax, jax.numpy as jnp
from jax import lax
import numpy as np

D_MODEL = 1024
BATCH = 2
SEQ = 8192
DEPTH = 1

HG_HEADS = 8
HG_DK = 128
HG_DV = D_MODEL // HG_HEADS
HG_KW = HG_HEADS * HG_DK
HG_VW = HG_HEADS * HG_DV
CHUNK = 64
CV_WIDTH = D_MODEL
CV_GROUPS = 8
CONV_K = 3
PROJ_WIDTHS = (HG_KW, HG_KW, HG_VW, HG_VW, CV_WIDTH, CV_WIDTH, CV_WIDTH, D_MODEL, D_MODEL)
PROJ_W = sum(PROJ_WIDTHS)
SPLIT_IDX = tuple(int(v) for v in np.cumsum(PROJ_WIDTHS)[:-1])
N_KEYS = 128
N_EXPERTS = N_KEYS * N_KEYS
PK_HEADS = 8
PK_DKEY = 256
PK_HALF = PK_DKEY // 2
PK_TOPK = 16
PEER_BLOCK = 128
PLE_DIM = 256
LN_EPS = 1e-5
RMS_EPS = 1e-6
ALPHA = (2.0 * DEPTH) ** 0.25
BETA = (8.0 * DEPTH) ** -0.25

kernel_name = "hybrid_hgrn2_shortconv_peer_deepnorm"


def layer_norm(x, g, b):
    xf = x.astype(jnp.float32)
    mu = jnp.mean(xf, axis=-1, keepdims=True)
    xc = xf - mu
    var = jnp.mean(xc * xc, axis=-1, keepdims=True)
    return (xc * lax.rsqrt(var + LN_EPS) * g.astype(jnp.float32) + b.astype(jnp.float32)).astype(x.dtype)


def hgrn2_chunkwise(q, k, v, logf):
    bsz, seq = q.shape[0], q.shape[1]
    n = seq // CHUNK

    def to_chunks(t):
        return t.astype(jnp.float32).reshape(bsz, n, CHUNK, HG_HEADS, t.shape[-1]).transpose(0, 3, 1, 2, 4)

    q, k, v, logf = to_chunks(q), to_chunks(k), to_chunks(v), to_chunks(logf)
    bcum = jnp.cumsum(logf, axis=3)
    qe = q * jnp.exp(bcum)
    ke = k * jnp.exp(-bcum)
    causal = jnp.tril(jnp.ones((CHUNK, CHUNK), dtype=bool))
    a = jnp.einsum('bhnik,bhnjk->bhnij', qe, ke)
    a = jnp.where(causal, a, 0.0)
    o_intra = jnp.einsum('bhnij,bhnjv->bhniv', a, v)
    b_last = bcum[:, :, :, -1:, :]
    k_end = k * jnp.exp(b_last - bcum)
    ds = jnp.einsum('bhnck,bhncv->bhnkv', k_end, v)
    decay = jnp.exp(b_last[:, :, :, 0, :])

    def step(state, inp):
        d, dsn = inp
        return d[..., None] * state + dsn, state

    s0 = jnp.zeros((bsz, HG_HEADS, HG_DK, HG_DV), jnp.float32)
    _, s_prev = lax.scan(step, s0, (jnp.moveaxis(decay, 2, 0), jnp.moveaxis(ds, 2, 0)))
    s_prev = jnp.moveaxis(s_prev, 0, 2)
    o_inter = jnp.einsum('bhnck,bhnkv->bhncv', qe, s_prev)
    o = o_intra + o_inter
    return o.transpose(0, 2, 3, 1, 4).reshape(bsz, seq, HG_HEADS, HG_DV)


def causal_depthwise_conv3(u, w, b):
    seq = u.shape[1]
    up = jnp.pad(u, ((0, 0), (CONV_K - 1, 0), (0, 0)))
    y = b
    for j in range(CONV_K):
        y = y + up[:, j:j + seq, :] * w[j]
    return y


def token_mixer(x, w_in, lb, hg_norm_g, conv_w, conv_b, w_branch_a, w_branch_b, w_out):
    bsz, seq = x.shape[0], x.shape[1]
    z = x @ w_in
    q, fz, iv, og, cb, cc, ch, ga, gb = jnp.split(z, SPLIT_IDX, axis=-1)
    q = q.reshape(bsz, seq, HG_HEADS, HG_DK) * (HG_DK ** -0.5)
    fz = fz.reshape(bsz, seq, HG_HEADS, HG_DK).astype(jnp.float32)
    lbh = lb.reshape(HG_HEADS, HG_DK)
    f = lbh + (1.0 - lbh) * jax.nn.sigmoid(fz)
    logf = jnp.log(f)
    kin = (1.0 - lbh) * jax.nn.sigmoid(-fz)
    v = iv.reshape(bsz, seq, HG_HEADS, HG_DV)
    o = hgrn2_chunkwise(q, kin, v, logf)
    o = o * lax.rsqrt(jnp.mean(o * o, axis=-1, keepdims=True) + RMS_EPS)
    o = o * hg_norm_g.reshape(HG_HEADS, HG_DV).astype(jnp.float32)
    o = o.reshape(bsz, seq, HG_VW).astype(x.dtype) * jax.nn.silu(og)
    yc = cb * causal_depthwise_conv3(cc * ch, conv_w, conv_b)
    merged = jax.nn.sigmoid(ga) * (o @ w_branch_a) + jax.nn.sigmoid(gb) * (yc @ w_branch_b)
    return merged @ w_out


def peer(x, w_q, sub_keys, u, v):
    bsz, seq, d = x.shape
    xt = x.reshape((bsz * seq) // PEER_BLOCK, PEER_BLOCK, d)

    def block(xb):
        tb = xb.shape[0]
        qh = (xb @ w_q).reshape(tb, PK_HEADS, 2, PK_HALF)
        s = jnp.einsum('thpd,hpkd->thpk', qh, sub_keys).astype(jnp.float32)
        s1, i1 = lax.top_k(s[:, :, 0], PK_TOPK)
        s2, i2 = lax.top_k(s[:, :, 1], PK_TOPK)
        cand = (s1[..., :, None] + s2[..., None, :]).reshape(tb, PK_HEADS, PK_TOPK * PK_TOPK)
        cidx = (i1[..., :, None] * N_KEYS + i2[..., None, :]).reshape(tb, PK_HEADS, PK_TOPK * PK_TOPK)
        top, pos = lax.top_k(cand, PK_TOPK)
        eidx = jnp.take_along_axis(cidx, pos, axis=-1)
        g = jax.nn.softmax(top, axis=-1).astype(xb.dtype)
        ue = jnp.take(u, eidx, axis=0)
        h = jax.nn.gelu(jnp.einsum('thkd,td->thk', ue, xb), approximate=False)
        ve = jnp.take(v, eidx, axis=0)
        return jnp.einsum('thk,thkd->td', g * h, ve)

    out = lax.map(block, xt)
    return out.reshape(bsz, seq, d)


def setup_inputs(seed: int = 0) -> dict:
    key = jax.random.key(seed)
    ks = jax.random.split(key, 20)
    f32 = jnp.float32
    nrm = lambda k, shape, s: jax.random.normal(k, shape, f32) * s
    return {
        "x": nrm(ks[0], (BATCH, SEQ, D_MODEL), 1.0),
        "p": nrm(ks[1], (DEPTH, BATCH, SEQ, PLE_DIM), 1.0),
        "w_in": nrm(ks[2], (DEPTH, D_MODEL, PROJ_W), D_MODEL ** -0.5),
        "lb_param": nrm(ks[3], (DEPTH + 1, HG_KW), 0.5),
        "hg_norm_g": 1.0 + nrm(ks[4], (DEPTH, HG_VW), 0.02),
        "conv_w": nrm(ks[5], (DEPTH, CONV_K, CV_WIDTH), CONV_K ** -0.5),
        "conv_b": nrm(ks[6], (DEPTH, CV_WIDTH), 0.02),
        "w_branch_a": nrm(ks[7], (DEPTH, HG_VW, D_MODEL), BETA * HG_VW ** -0.5),
        "w_branch_b": nrm(ks[8], (DEPTH, CV_WIDTH, D_MODEL), BETA * CV_WIDTH ** -0.5),
        "w_out": nrm(ks[9], (DEPTH, D_MODEL, D_MODEL), BETA * D_MODEL ** -0.5),
        "ln1_g": 1.0 + nrm(ks[10], (DEPTH, D_MODEL), 0.02),
        "ln1_b": nrm(ks[11], (DEPTH, D_MODEL), 0.02),
        "pk_w_q": nrm(ks[12], (DEPTH, D_MODEL, PK_HEADS * PK_DKEY), D_MODEL ** -0.5),
        "pk_sub_keys": nrm(ks[13], (DEPTH, PK_HEADS, 2, N_KEYS, PK_HALF), PK_HALF ** -0.5),
        "pk_u": nrm(ks[14], (DEPTH, N_EXPERTS, D_MODEL), D_MODEL ** -0.5),
        "pk_v": nrm(ks[15], (DEPTH, N_EXPERTS, D_MODEL), BETA),
        "ln2_g": 1.0 + nrm(ks[16], (DEPTH, D_MODEL), 0.02),
        "ln2_b": nrm(ks[17], (DEPTH, D_MODEL), 0.02),
        "ple_w_gate": nrm(ks[18], (DEPTH, D_MODEL, D_MODEL), D_MODEL ** -0.5),
        "ple_w_proj": nrm(ks[19], (DEPTH, PLE_DIM, D_MODEL), PLE_DIM ** -0.5),
    }


def reference(x, p, w_in, lb_param, hg_norm_g, conv_w, conv_b, w_branch_a, w_branch_b, w_out,
              ln1_g, ln1_b, pk_w_q, pk_sub_keys, pk_u, pk_v, ln2_g, ln2_b, ple_w_gate, ple_w_proj):
    lb_all = jnp.cumsum(jax.nn.softmax(lb_param.astype(jnp.float32), axis=0), axis=0)
    for i in range(DEPTH):
        mix = token_mixer(x, w_in[i], lb_all[i], hg_norm_g[i], conv_w[i], conv_b[i],
                          w_branch_a[i], w_branch_b[i], w_out[i])
        x = layer_norm(ALPHA * x + mix, ln1_g[i], ln1_b[i])
        x = layer_norm(ALPHA * x + peer(x, pk_w_q[i], pk_sub_keys[i], pk_u[i], pk_v[i]), ln2_g[i], ln2_b[i])
        x = x + jax.nn.sigmoid(x @ ple_w_gate[i]) * (p[i] @ ple_w_proj[i])
    return x
```

```python
import functools
import math

import jax
import jax.numpy as jnp
from jax import lax
from jax.experimental import pallas as pl
from jax.experimental.pallas import tpu as pltpu

D_MODEL = 1024
HEADS = 8
HEAD_DIM = 128
CHUNK = 64
LANES = 128
N_PLANES = 72
N_KEYS = 128
PK_HEADS = 8
PK_TOPK = 16
N_EXPERTS = N_KEYS * N_KEYS
PLE_DIM = 256
LN_EPS = 1e-5
RMS_EPS = 1e-6
DEPTH = 1
ALPHA = (2.0 * DEPTH) ** 0.25
VMEM_LIMIT_BYTES = 56 * 1024 * 1024

F32 = jnp.float32
BF16 = jnp.bfloat16
NT_DIMS = (((1,), (1,)), ((), ()))
TN_DIMS = (((0,), (0,)), ((), ()))

PROJ_TM, PROJ_TN = 512, 1024
HGRN_TS, HGRN_SUB = 512, 256
MIX_TM = 256
TOPK_TM = 256
DENSE_TM, DENSE_TE = 512, 512
FINAL_TM = 256


def _params(*sem):
    return pltpu.CompilerParams(dimension_semantics=sem, vmem_limit_bytes=VMEM_LIMIT_BYTES)


def _layer_norm(y, g, b):
    mu = jnp.mean(y, axis=-1, keepdims=True)
    yc = y - mu
    var = jnp.mean(yc * yc, axis=-1, keepdims=True)
    return yc * lax.rsqrt(var + LN_EPS) * g + b


def _proj_kernel(x_ref, w_ref, z_ref):
    acc = jnp.dot(x_ref[...], w_ref[...], preferred_element_type=F32)
    for c in range(PROJ_TN // LANES):
        z_ref[c] = acc[:, c * LANES:(c + 1) * LANES]


def _proj(x_bf, w_bf):
    t = x_bf.shape[0]
    n = w_bf.shape[1]
    return pl.pallas_call(
        _proj_kernel,
        grid=(n // PROJ_TN, t // PROJ_TM),
        in_specs=[pl.BlockSpec((PROJ_TM, D_MODEL), lambda j, i: (i, 0)),
                  pl.BlockSpec((D_MODEL, PROJ_TN), lambda j, i: (0, j))],
        out_specs=pl.BlockSpec((PROJ_TN // LANES, PROJ_TM, LANES), lambda j, i: (j, i, 0)),
        out_shape=jax.ShapeDtypeStruct((n // LANES, t, LANES), F32),
        compiler_params=_params("parallel", "parallel"),
        name="proj",
    )(x_bf, w_bf)


def _hgrn_kernel(q_ref, f_ref, v_ref, og_ref, lbp_ref, g_ref, o_ref, st_ref):
    @pl.when(pl.program_id(2) == 0)
    def _():
        st_ref[...] = jnp.zeros_like(st_ref)

    lbp = lbp_ref[...]
    lbe = jnp.exp(lbp - jnp.max(lbp, axis=0, keepdims=True))
    lb = lbe[0:1] / jnp.sum(lbe, axis=0, keepdims=True)

    r = lax.broadcasted_iota(jnp.int32, (HGRN_SUB, HGRN_SUB), 0)
    c = lax.broadcasted_iota(jnp.int32, (HGRN_SUB, HGRN_SUB), 1)
    causal = ((r // CHUNK) == (c // CHUNK)) & (c <= r)
    tril = causal.astype(BF16)
    scale = HEAD_DIM ** -0.5

    for s in range(HGRN_TS // HGRN_SUB):
        rows = pl.ds(s * HGRN_SUB, HGRN_SUB)
        fz = f_ref[rows, :]
        q = q_ref[rows, :] * scale
        v = v_ref[rows, :]
        f = lb + (1.0 - lb) * jax.nn.sigmoid(fz)
        logf = jnp.log(f)
        k = (1.0 - lb) * jax.nn.sigmoid(-fz)
        p0 = logf.astype(BF16)
        r0 = logf - p0.astype(F32)
        p1 = r0.astype(BF16)
        p2 = (r0 - p1.astype(F32)).astype(BF16)
        bcum = (jnp.dot(tril, p0, preferred_element_type=F32)
                + jnp.dot(tril, p1, preferred_element_type=F32)
                + jnp.dot(tril, p2, preferred_element_type=F32))
        qe = (q * jnp.exp(bcum)).astype(BF16)
        ke = (k * jnp.exp(-bcum)).astype(BF16)
        v_bf = v.astype(BF16)
        a = lax.dot_general(qe, ke, NT_DIMS, preferred_element_type=F32)
        a = jnp.where(causal, a, 0.0).astype(BF16)
        o = jnp.dot(a, v_bf, preferred_element_type=F32)
        outs = []
        for ci in range(HGRN_SUB // CHUNK):
            lo, hi = ci * CHUNK, (ci + 1) * CHUNK
            b_last = bcum[hi - 1:hi, :]
            k_end = (k[lo:hi] * jnp.exp(b_last - bcum[lo:hi])).astype(BF16)
            st = st_ref[...]
            o_inter = lax.dot_general(qe[lo:hi], st.astype(BF16), NT_DIMS, preferred_element_type=F32)
            ds_t = lax.dot_general(v_bf[lo:hi], k_end, TN_DIMS, preferred_element_type=F32)
            st_ref[...] = st * jnp.exp(b_last) + ds_t
            outs.append(o[lo:hi] + o_inter)
        o = jnp.concatenate(outs, axis=0)
        o = o * lax.rsqrt(jnp.mean(o * o, axis=-1, keepdims=True) + RMS_EPS) * g_ref[...]
        og = og_ref[rows, :]
        o_ref[rows, :] = (o * (og * jax.nn.sigmoid(og))).astype(BF16)


def _hgrn(z3, lb_param, norm_g, batch, seq):
    nt = seq // HGRN_TS

    def plane(group):
        return pl.BlockSpec((None, HGRN_TS, LANES), lambda b, h, n: (group * HEADS + h, b * nt + n, 0))

    return pl.pallas_call(
        _hgrn_kernel,
        grid=(batch, HEADS, nt),
        in_specs=[plane(0), plane(1), plane(2), plane(3),
                  pl.BlockSpec((DEPTH + 1, LANES), lambda b, h, n: (0, h)),
                  pl.BlockSpec((1, LANES), lambda b, h, n: (0, h))],
        out_specs=pl.BlockSpec((HGRN_TS, LANES), lambda b, h, n: (b * nt + n, h)),
        out_shape=jax.ShapeDtypeStruct((batch * seq, D_MODEL), BF16),
        scratch_shapes=[pltpu.VMEM((HEAD_DIM, HEAD_DIM), F32)],
        compiler_params=_params("parallel", "parallel", "arbitrary"),
        name="hgrn",
    )(z3, z3, z3, z3, lb_param, norm_g)


def _mix_kernel(o_ref, cb_ref, cc_ref, ch_ref, ga_ref, gb_ref, ccp_ref, chp_ref, x_ref,
                cw_ref, cbias_ref, wa_ref, wb_ref, wo_ref, g_ref, b_ref, x1_ref, x1b_ref, *, tiles_per_seq):
    first = (pl.program_id(0) % tiles_per_seq) == 0
    row = lax.broadcasted_iota(jnp.int32, (MIX_TM, LANES), 0)
    ycs = []
    for c in range(HEADS):
        u = cc_ref[c] * ch_ref[c]
        halo = jnp.where(first, 0.0, ccp_ref[c] * chp_ref[c])
        h1 = halo[7:8, :]
        h2 = halo[6:7, :]
        u1 = jnp.where(row == 0, h1, pltpu.roll(u, 1, 0))
        u2 = jnp.where(row == 0, h2, jnp.where(row == 1, h1, pltpu.roll(u, 2, 0)))
        cs = slice(c * LANES, (c + 1) * LANES)
        y = cbias_ref[:, cs] + cw_ref[0:1, cs] * u2 + cw_ref[1:2, cs] * u1 + cw_ref[2:3, cs] * u
        ycs.append((cb_ref[c] * y).astype(BF16))
    yc = jnp.concatenate(ycs, axis=1)
    ga = jnp.concatenate([ga_ref[c] for c in range(HEADS)], axis=1)
    gb = jnp.concatenate([gb_ref[c] for c in range(HEADS)], axis=1)
    ta = jnp.dot(o_ref[...], wa_ref[...], preferred_element_type=F32)
    tb = jnp.dot(yc, wb_ref[...], preferred_element_type=F32)
    merged = jax.nn.sigmoid(ga) * ta + jax.nn.sigmoid(gb) * tb
    mix = jnp.dot(merged.astype(BF16), wo_ref[...], preferred_element_type=F32)
    x1 = _layer_norm(ALPHA * x_ref[...] + mix, g_ref[...], b_ref[...])
    x1_ref[...] = x1
    x1b_ref[...] = x1.astype(BF16)


def _mix(o_g, z3, x2d, conv_w, conv_b, wa, wb, wo, ln_g, ln_b, seq):
    t = x2d.shape[0]
    tiles_per_seq = seq // MIX_TM
    sub_per_tile = MIX_TM // 8

    def planes(group):
        return pl.BlockSpec((HEADS, MIX_TM, LANES), lambda i: (group, i, 0))

    def prev_rows(group):
        return pl.BlockSpec((HEADS, 8, LANES), lambda i: (group, jnp.maximum(i * sub_per_tile - 1, 0), 0))

    def full(shape):
        return pl.BlockSpec(shape, lambda i: (0,) * len(shape))

    row_blk = pl.BlockSpec((MIX_TM, D_MODEL), lambda i: (i, 0))
    return pl.pallas_call(
        functools.partial(_mix_kernel, tiles_per_seq=tiles_per_seq),
        grid=(t // MIX_TM,),
        in_specs=[row_blk, planes(4), planes(5), planes(6), planes(7), planes(8),
                  prev_rows(5), prev_rows(6), row_blk,
                  full((3, D_MODEL)), full((1, D_MODEL)),
                  full((D_MODEL, D_MODEL)), full((D_MODEL, D_MODEL)), full((D_MODEL, D_MODEL)),
                  full((1, D_MODEL)), full((1, D_MODEL))],
        out_specs=[row_blk, row_blk],
        out_shape=[jax.ShapeDtypeStruct((t, D_MODEL), F32), jax.ShapeDtypeStruct((t, D_MODEL), BF16)],
        compiler_params=_params("parallel"),
        name="mix",
    )(o_g, z3, z3, z3, z3, z3, z3, z3, x2d, conv_w, conv_b, wa, wb, wo, ln_g, ln_b)


def _top_values(s, k):
    rows = []
    for _ in range(k):
        m = jnp.max(s, axis=0, keepdims=True)
        rows.append(m)
        s = jnp.where(s == m, -jnp.inf, s)
    return rows


def _topk_kernel(x_ref, wq_ref, keys_ref, s2_ref, e2_ref, e1_ref, thr_ref):
    q_t = lax.dot_general(wq_ref[...], x_ref[...], NT_DIMS, preferred_element_type=F32)
    for h in range(PK_HEADS):
        q1 = q_t[(2 * h) * LANES:(2 * h + 1) * LANES].astype(BF16)
        q2 = q_t[(2 * h + 1) * LANES:(2 * h + 2) * LANES].astype(BF16)
        s1 = jnp.dot(keys_ref[2 * h], q1, preferred_element_type=F32)
        s2 = jnp.dot(keys_ref[2 * h + 1], q2, preferred_element_type=F32)
        a = _top_values(s1, PK_TOPK)
        b = _top_values(s2, PK_TOPK)
        b_all = jnp.concatenate(b, axis=0)
        cand = jnp.concatenate([a[r1] + b_all for r1 in range(PK_TOPK)], axis=0)
        tau = _top_values(cand, PK_TOPK)[-1]
        top = a[0] + b[0]
        z = jnp.sum(jnp.where(cand >= tau, jnp.exp(cand - top), 0.0), axis=0, keepdims=True)
        thr = jnp.full(s1.shape, jnp.inf, F32)
        for r2 in range(PK_TOPK):
            thr = jnp.where((s1 + b[r2]) >= tau, b[r2], thr)
        s2_ref[h] = s2
        e2_ref[h] = jnp.exp(s2 - b[0])
        e1_ref[h] = jnp.exp(s1 - a[0]) / z
        thr_ref[h] = thr


def _topk(x1b, wq_t, keys):
    t = x1b.shape[0]
    side = pl.BlockSpec((PK_HEADS, N_KEYS, TOPK_TM), lambda i: (0, 0, i))
    side_shape = jax.ShapeDtypeStruct((PK_HEADS, N_KEYS, t), F32)
    return pl.pallas_call(
        _topk_kernel,
        grid=(t // TOPK_TM,),
        in_specs=[pl.BlockSpec((TOPK_TM, D_MODEL), lambda i: (i, 0)),
                  pl.BlockSpec(wq_t.shape, lambda i: (0, 0)),
                  pl.BlockSpec(keys.shape, lambda i: (0, 0, 0))],
        out_specs=[side, side, side, side],
        out_shape=[side_shape, side_shape, side_shape, side_shape],
        compiler_params=_params("parallel"),
        name="topk",
    )(x1b, wq_t, keys)


def _dense_kernel(x_ref, u_ref, vt_ref, s2_ref, e2_ref, e1_ref, thr_ref, o_ref):
    j = pl.program_id(1)

    @pl.when(j == 0)
    def _():
        o_ref[...] = jnp.zeros_like(o_ref)

    h_t = lax.dot_general(u_ref[...], x_ref[...], NT_DIMS, preferred_element_type=F32)
    act = 0.5 * h_t * (1.0 + lax.erf(h_t * (1.0 / math.sqrt(2.0))))
    parts = []
    for a in range(DENSE_TE // N_KEYS):
        key1 = j * (DENSE_TE // N_KEYS) + a
        w = jnp.zeros((N_KEYS, DENSE_TM), F32)
        for h in range(PK_HEADS):
            e1 = e1_ref[h, pl.ds(key1, 1), :]
            th = thr_ref[h, pl.ds(key1, 1), :]
            w = w + jnp.where(s2_ref[h] >= th, e2_ref[h] * e1, 0.0)
        parts.append((w * act[a * N_KEYS:(a + 1) * N_KEYS]).astype(BF16))
    a_t = jnp.concatenate(parts, axis=0)
    o_ref[...] += jnp.dot(vt_ref[...], a_t, preferred_element_type=F32)


def _dense(x1b, u_bf, vt_bf, s2, e2, e1, thr):
    t = x1b.shape[0]
    side = pl.BlockSpec((PK_HEADS, N_KEYS, DENSE_TM), lambda i, j: (0, 0, i))
    return pl.pallas_call(
        _dense_kernel,
        grid=(t // DENSE_TM, N_EXPERTS // DENSE_TE),
        in_specs=[pl.BlockSpec((DENSE_TM, D_MODEL), lambda i, j: (i, 0)),
                  pl.BlockSpec((DENSE_TE, D_MODEL), lambda i, j: (j, 0)),
                  pl.BlockSpec((D_MODEL, DENSE_TE), lambda i, j: (0, j)),
                  side, side, side, side],
        out_specs=pl.BlockSpec((D_MODEL, DENSE_TM), lambda i, j: (0, i)),
        out_shape=jax.ShapeDtypeStruct((D_MODEL, t), F32),
        compiler_params=_params("parallel", "arbitrary"),
        name="dense",
    )(x1b, u_bf, vt_bf, s2, e2, e1, thr)


def _final_kernel(pt_ref, x1_ref, p_ref, g_ref, b_ref, wg_ref, wp_ref, out_ref):
    peer = pt_ref[...].T
    x2 = _layer_norm(ALPHA * x1_ref[...] + peer, g_ref[...], b_ref[...])
    gate = jax.nn.sigmoid(jnp.dot(x2.astype(BF16), wg_ref[...], preferred_element_type=F32))
    proj = jnp.dot(p_ref[...].astype(BF16), wp_ref[...], preferred_element_type=F32)
    out_ref[...] = x2 + gate * proj


def _final(peer_t, x1, p2d, ln_g, ln_b, wg, wp):
    t = x1.shape[0]

    def full(shape):
        return pl.BlockSpec(shape, lambda i: (0,) * len(shape))

    row_blk = pl.BlockSpec((FINAL_TM, D_MODEL), lambda i: (i, 0))
    return pl.pallas_call(
        _final_kernel,
        grid=(t // FINAL_TM,),
        in_specs=[pl.BlockSpec((D_MODEL, FINAL_TM), lambda i: (0, i)), row_blk,
                  pl.BlockSpec((FINAL_TM, PLE_DIM), lambda i: (i, 0)),
                  full((1, D_MODEL)), full((1, D_MODEL)),
                  full((D_MODEL, D_MODEL)), full((PLE_DIM, D_MODEL))],
        out_specs=row_blk,
        out_shape=jax.ShapeDtypeStruct((t, D_MODEL), F32),
        compiler_params=_params("parallel"),
        name="final",
    )(peer_t, x1, p2d, ln_g, ln_b, wg, wp)


def kernel(x, p, w_in, lb_param, hg_norm_g, conv_w, conv_b, w_branch_a, w_branch_b, w_out, ln1_g, ln1_b,
           pk_w_q, pk_sub_keys, pk_u, pk_v, ln2_g, ln2_b, ple_w_gate, ple_w_proj):
    batch, seq, d = x.shape
    assert d == D_MODEL and w_in.shape[0] == DEPTH and lb_param.shape[0] == DEPTH + 1
    assert seq % HGRN_TS == 0 and seq % MIX_TM == 0
    t = batch * seq
    x2d = x.reshape(t, d)

    z3 = _proj(x2d.astype(BF16), w_in[0].astype(BF16))
    o_g = _hgrn(z3, lb_param, hg_norm_g, batch, seq)
    x1, x1b = _mix(o_g, z3, x2d, conv_w[0], conv_b, w_branch_a[0].astype(BF16), w_branch_b[0].astype(BF16),
                   w_out[0].astype(BF16), ln1_g, ln1_b, seq)

    wq_t = pk_w_q[0].T.astype(BF16)
    keys = pk_sub_keys[0].reshape(PK_HEADS * 2, N_KEYS, N_KEYS).astype(BF16)
    s2, e2, e1, thr = _topk(x1b, wq_t, keys)
    peer_t = _dense(x1b, pk_u[0].astype(BF16), pk_v[0].T.astype(BF16), s2, e2, e1, thr)

    out = _final(peer_t, x1, p[0].reshape(t, PLE_DIM), ln2_g, ln2_b,
                 ple_w_gate[0].astype(BF16), ple_w_proj[0].astype(BF16))
    return out.reshape(batch, seq, d)
```

```python
import functools
import math

import jax
import jax.numpy as jnp
from jax import lax
from jax.experimental import pallas as pl
from jax.experimental.pallas import tpu as pltpu

D_MODEL = 1024
HEADS = 8
HEAD_DIM = 128
CHUNK = 64
LANES = 128
BF16_ROWS = 16
N_PLANES = 72
N_KEYS = 128
PK_HEADS = 8
PK_TOPK = 16
N_EXPERTS = N_KEYS * N_KEYS
PLE_DIM = 256
LN_EPS = 1e-5
RMS_EPS = 1e-6
DEPTH = 1
ALPHA = (2.0 * DEPTH) ** 0.25
VMEM_LIMIT_BYTES = 56 * 1024 * 1024

F32 = jnp.float32
BF16 = jnp.bfloat16
NT_DIMS = (((1,), (1,)), ((), ()))
TN_DIMS = (((0,), (0,)), ((), ()))

PROJ_TM, PROJ_TN = 512, 1024
HGRN_TS, HGRN_SUB = 512, 256
MIX_TM = 256
TOPK_TM = 256
DENSE_TM, DENSE_TE = 512, 512
FINAL_TM = 256


def _params(*sem):
    return pltpu.CompilerParams(dimension_semantics=sem, vmem_limit_bytes=VMEM_LIMIT_BYTES)


def _layer_norm(y, g, b):
    mu = jnp.mean(y, axis=-1, keepdims=True)
    yc = y - mu
    var = jnp.mean(yc * yc, axis=-1, keepdims=True)
    return yc * lax.rsqrt(var + LN_EPS) * g + b


def _proj_kernel(x_ref, w_ref, z_ref):
    acc = jnp.dot(x_ref[...], w_ref[...], preferred_element_type=F32)
    for c in range(PROJ_TN // LANES):
        z_ref[c] = acc[:, c * LANES:(c + 1) * LANES]


def _proj(x_bf, w_bf):
    t = x_bf.shape[0]
    n = w_bf.shape[1]
    return pl.pallas_call(
        _proj_kernel,
        grid=(n // PROJ_TN, t // PROJ_TM),
        in_specs=[pl.BlockSpec((PROJ_TM, D_MODEL), lambda j, i: (i, 0)),
                  pl.BlockSpec((D_MODEL, PROJ_TN), lambda j, i: (0, j))],
        out_specs=pl.BlockSpec((PROJ_TN // LANES, PROJ_TM, LANES), lambda j, i: (j, i, 0)),
        out_shape=jax.ShapeDtypeStruct((n // LANES, t, LANES), F32),
        compiler_params=_params("parallel", "parallel"),
        name="proj",
    )(x_bf, w_bf)


def _hgrn_kernel(q_ref, f_ref, v_ref, og_ref, lbp_ref, g_ref, o_ref, st_ref):
    @pl.when(pl.program_id(2) == 0)
    def _():
        st_ref[...] = jnp.zeros_like(st_ref)

    lbp = lbp_ref[...]
    lbe = jnp.exp(lbp - jnp.max(lbp, axis=0, keepdims=True))
    lb = lbe[0:1] / jnp.sum(lbe, axis=0, keepdims=True)

    r = lax.broadcasted_iota(jnp.int32, (HGRN_SUB, HGRN_SUB), 0)
    c = lax.broadcasted_iota(jnp.int32, (HGRN_SUB, HGRN_SUB), 1)
    causal = ((r // CHUNK) == (c // CHUNK)) & (c <= r)
    tril = causal.astype(BF16)
    scale = HEAD_DIM ** -0.5

    for s in range(HGRN_TS // HGRN_SUB):
        rows = pl.ds(s * HGRN_SUB, HGRN_SUB)
        fz = f_ref[rows, :]
        q = q_ref[rows, :] * scale
        v = v_ref[rows, :]
        f = lb + (1.0 - lb) * jax.nn.sigmoid(fz)
        logf = jnp.log(f)
        k = (1.0 - lb) * jax.nn.sigmoid(-fz)
        p0 = logf.astype(BF16)
        r0 = logf - p0.astype(F32)
        p1 = r0.astype(BF16)
        p2 = (r0 - p1.astype(F32)).astype(BF16)
        bcum = (jnp.dot(tril, p0, preferred_element_type=F32)
                + jnp.dot(tril, p1, preferred_element_type=F32)
                + jnp.dot(tril, p2, preferred_element_type=F32))
        qe = (q * jnp.exp(bcum)).astype(BF16)
        ke = (k * jnp.exp(-bcum)).astype(BF16)
        v_bf = v.astype(BF16)
        a = lax.dot_general(qe, ke, NT_DIMS, preferred_element_type=F32)
        a = jnp.where(causal, a, 0.0).astype(BF16)
        o = jnp.dot(a, v_bf, preferred_element_type=F32)
        outs = []
        for ci in range(HGRN_SUB // CHUNK):
            lo, hi = ci * CHUNK, (ci + 1) * CHUNK
            b_last = bcum[hi - 1:hi, :]
            k_end = (k[lo:hi] * jnp.exp(b_last - bcum[lo:hi])).astype(BF16)
            st = st_ref[...]
            o_inter = lax.dot_general(qe[lo:hi], st.astype(BF16), NT_DIMS, preferred_element_type=F32)
            ds_t = lax.dot_general(v_bf[lo:hi], k_end, TN_DIMS, preferred_element_type=F32)
            st_ref[...] = st * jnp.exp(b_last) + ds_t
            outs.append(o[lo:hi] + o_inter)
        o = jnp.concatenate(outs, axis=0)
        o = o * lax.rsqrt(jnp.mean(o * o, axis=-1, keepdims=True) + RMS_EPS) * g_ref[...]
        og = og_ref[rows, :]
        o_ref[rows, :] = (o * (og * jax.nn.sigmoid(og))).astype(BF16)


def _hgrn(z3, lb_param, norm_g, batch, seq):
    nt = seq // HGRN_TS

    def plane(group):
        return pl.BlockSpec((None, HGRN_TS, LANES), lambda b, h, n: (group * HEADS + h, b * nt + n, 0))

    return pl.pallas_call(
        _hgrn_kernel,
        grid=(batch, HEADS, nt),
        in_specs=[plane(0), plane(1), plane(2), plane(3),
                  pl.BlockSpec((DEPTH + 1, LANES), lambda b, h, n: (0, h)),
                  pl.BlockSpec((1, LANES), lambda b, h, n: (0, h))],
        out_specs=pl.BlockSpec((HGRN_TS, LANES), lambda b, h, n: (b * nt + n, h)),
        out_shape=jax.ShapeDtypeStruct((batch * seq, D_MODEL), BF16),
        scratch_shapes=[pltpu.VMEM((HEAD_DIM, HEAD_DIM), F32)],
        compiler_params=_params("parallel", "parallel", "arbitrary"),
        name="hgrn",
    )(z3, z3, z3, z3, lb_param, norm_g)


def _mix_kernel(o_ref, cb_ref, cc_ref, ch_ref, ga_ref, gb_ref, ccp_ref, chp_ref, x_ref,
                cw_ref, cbias_ref, wa_ref, wb_ref, wo_ref, g_ref, b_ref, x1_ref, x1b_ref, *, tiles_per_seq):
    first = (pl.program_id(0) % tiles_per_seq) == 0
    row = lax.broadcasted_iota(jnp.int32, (MIX_TM, LANES), 0)
    ycs = []
    for c in range(HEADS):
        u = cc_ref[c] * ch_ref[c]
        halo = jnp.where(first, 0.0, ccp_ref[c] * chp_ref[c])
        h1 = halo[7:8, :]
        h2 = halo[6:7, :]
        u1 = jnp.where(row == 0, h1, pltpu.roll(u, 1, 0))
        u2 = jnp.where(row == 0, h2, jnp.where(row == 1, h1, pltpu.roll(u, 2, 0)))
        cs = slice(c * LANES, (c + 1) * LANES)
        y = cbias_ref[:, cs] + cw_ref[0:1, cs] * u2 + cw_ref[1:2, cs] * u1 + cw_ref[2:3, cs] * u
        ycs.append((cb_ref[c] * y).astype(BF16))
    yc = jnp.concatenate(ycs, axis=1)
    ga = jnp.concatenate([ga_ref[c] for c in range(HEADS)], axis=1)
    gb = jnp.concatenate([gb_ref[c] for c in range(HEADS)], axis=1)
    ta = jnp.dot(o_ref[...], wa_ref[...], preferred_element_type=F32)
    tb = jnp.dot(yc, wb_ref[...], preferred_element_type=F32)
    merged = jax.nn.sigmoid(ga) * ta + jax.nn.sigmoid(gb) * tb
    mix = jnp.dot(merged.astype(BF16), wo_ref[...], preferred_element_type=F32)
    x1 = _layer_norm(ALPHA * x_ref[...] + mix, g_ref[...], b_ref[...])
    x1_ref[...] = x1
    x1b_ref[...] = x1.astype(BF16)


def _mix(o_g, z3, x2d, conv_w, conv_b, wa, wb, wo, ln_g, ln_b, seq):
    t = x2d.shape[0]
    tiles_per_seq = seq // MIX_TM
    sub_per_tile = MIX_TM // 8

    def planes(group):
        return pl.BlockSpec((HEADS, MIX_TM, LANES), lambda i: (group, i, 0))

    def prev_rows(group):
        return pl.BlockSpec((HEADS, 8, LANES), lambda i: (group, jnp.maximum(i * sub_per_tile - 1, 0), 0))

    def full(shape):
        return pl.BlockSpec(shape, lambda i: (0,) * len(shape))

    row_blk = pl.BlockSpec((MIX_TM, D_MODEL), lambda i: (i, 0))
    return pl.pallas_call(
        functools.partial(_mix_kernel, tiles_per_seq=tiles_per_seq),
        grid=(t // MIX_TM,),
        in_specs=[row_blk, planes(4), planes(5), planes(6), planes(7), planes(8),
                  prev_rows(5), prev_rows(6), row_blk,
                  full((3, D_MODEL)), full((1, D_MODEL)),
                  full((D_MODEL, D_MODEL)), full((D_MODEL, D_MODEL)), full((D_MODEL, D_MODEL)),
                  full((1, D_MODEL)), full((1, D_MODEL))],
        out_specs=[row_blk, row_blk],
        out_shape=[jax.ShapeDtypeStruct((t, D_MODEL), F32), jax.ShapeDtypeStruct((t, D_MODEL), BF16)],
        compiler_params=_params("parallel"),
        name="mix",
    )(o_g, z3, z3, z3, z3, z3, z3, z3, x2d, conv_w, conv_b, wa, wb, wo, ln_g, ln_b)


def _top_values(s, k, with_rank=False):
    rows = []
    rank = jnp.full(s.shape, float(k), F32) if with_rank else None
    for r in range(k):
        m = jnp.max(s, axis=0, keepdims=True)
        rows.append(m)
        hit = s == m
        if with_rank:
            rank = jnp.where(hit, float(r), rank)
        s = jnp.where(hit, -jnp.inf, s)
    return (rows, rank) if with_rank else rows


def _topk_kernel(x_ref, wq_ref, keys_ref, r2_ref, e2_ref, e1_ref, c1_ref):
    q_t = lax.dot_general(wq_ref[...], x_ref[...], NT_DIMS, preferred_element_type=F32)
    for h in range(PK_HEADS):
        q1 = q_t[(2 * h) * LANES:(2 * h + 1) * LANES].astype(BF16)
        q2 = q_t[(2 * h + 1) * LANES:(2 * h + 2) * LANES].astype(BF16)
        s1 = jnp.dot(keys_ref[2 * h], q1, preferred_element_type=F32)
        s2 = jnp.dot(keys_ref[2 * h + 1], q2, preferred_element_type=F32)
        a = _top_values(s1, PK_TOPK)
        b, rank2 = _top_values(s2, PK_TOPK, with_rank=True)
        a_all = jnp.concatenate(a, axis=0)
        b_all = jnp.concatenate(b, axis=0)
        half = PK_TOPK // 2
        cand = jnp.concatenate([a[0] + b_all] + [a[r1] + b_all[:half] for r1 in range(1, half)]
                               + [a_all[half:] + b[0]], axis=0)
        tau = _top_values(cand, PK_TOPK)[-1]
        top = a[0] + b[0]
        z = jnp.sum(jnp.where(cand >= tau, jnp.exp(cand - top), 0.0), axis=0, keepdims=True)
        count1 = jnp.zeros(s1.shape, F32)
        for r2 in range(PK_TOPK):
            count1 = count1 + jnp.where((s1 + b[r2]) >= tau, 1.0, 0.0)
        r2_ref[h * N_KEYS:(h + 1) * N_KEYS, :] = rank2.astype(BF16)
        e2_ref[h * N_KEYS:(h + 1) * N_KEYS, :] = jnp.exp(s2 - b[0]).astype(BF16)
        e1_ref[h] = jnp.exp(s1 - a[0]) * (0.5 / z)
        c1_ref[h] = count1


def _topk(x1b, wq_t, keys):
    t = x1b.shape[0]
    side = pl.BlockSpec((PK_HEADS, N_KEYS, TOPK_TM), lambda i: (0, 0, i))
    f32_side = jax.ShapeDtypeStruct((PK_HEADS, N_KEYS, t), F32)
    side2 = pl.BlockSpec((PK_HEADS * N_KEYS, TOPK_TM), lambda i: (0, i))
    bf16_side = jax.ShapeDtypeStruct((PK_HEADS * N_KEYS, t), BF16)
    return pl.pallas_call(
        _topk_kernel,
        grid=(t // TOPK_TM,),
        in_specs=[pl.BlockSpec((TOPK_TM, D_MODEL), lambda i: (i, 0)),
                  pl.BlockSpec(wq_t.shape, lambda i: (0, 0)),
                  pl.BlockSpec(keys.shape, lambda i: (0, 0, 0))],
        out_specs=[side2, side2, side, side],
        out_shape=[bf16_side, bf16_side, f32_side, f32_side],
        compiler_params=_params("parallel"),
        name="topk",
    )(x1b, wq_t, keys)


def _bcast_rows_bf16(row):
    tile = jnp.broadcast_to(row, (BF16_ROWS, LANES)).astype(BF16)
    return jnp.broadcast_to(tile[None], (N_KEYS // BF16_ROWS, BF16_ROWS, LANES)).reshape(N_KEYS, LANES)


def _dense_kernel(x_ref, u_ref, vt_ref, r2_ref, e2_ref, e1_ref, c1_ref, o_ref, a_ref):
    j = pl.program_id(1)

    @pl.when(j == 0)
    def _():
        o_ref[...] = jnp.zeros_like(o_ref)

    n_key1 = DENSE_TE // N_KEYS
    e1_rows = [[e1_ref[h, pl.ds(j * n_key1 + a, 1), :] for h in range(PK_HEADS)] for a in range(n_key1)]
    c1_rows = [[c1_ref[h, pl.ds(j * n_key1 + a, 1), :] for h in range(PK_HEADS)] for a in range(n_key1)]
    h_t = lax.dot_general(u_ref[...], x_ref[...], NT_DIMS, preferred_element_type=F32)
    act = (h_t * (1.0 + lax.erf(h_t * (1.0 / math.sqrt(2.0))))).astype(BF16)
    for tc in range(DENSE_TM // LANES):
        lanes = slice(tc * LANES, (tc + 1) * LANES)
        for a in range(n_key1):
            w = jnp.zeros((N_KEYS, LANES), BF16)
            for h in range(PK_HEADS):
                e1 = _bcast_rows_bf16(e1_rows[a][h][:, lanes])
                c1 = _bcast_rows_bf16(c1_rows[a][h][:, lanes])
                keys2 = slice(h * N_KEYS, (h + 1) * N_KEYS)
                w = w + jnp.where(c1 > r2_ref[keys2, lanes], e1 * e2_ref[keys2, lanes], 0.0)
            a_ref[a * N_KEYS:(a + 1) * N_KEYS, lanes] = w * act[a * N_KEYS:(a + 1) * N_KEYS, lanes]
    o_ref[...] += jnp.dot(vt_ref[...], a_ref[...], preferred_element_type=F32)


def _dense(x1b, u_bf, vt_bf, r2, e2, e1, c1):
    t = x1b.shape[0]
    side = pl.BlockSpec((PK_HEADS, N_KEYS, DENSE_TM), lambda i, j: (0, 0, i))
    side2 = pl.BlockSpec((PK_HEADS * N_KEYS, DENSE_TM), lambda i, j: (0, i))
    return pl.pallas_call(
        _dense_kernel,
        grid=(t // DENSE_TM, N_EXPERTS // DENSE_TE),
        in_specs=[pl.BlockSpec((DENSE_TM, D_MODEL), lambda i, j: (i, 0)),
                  pl.BlockSpec((DENSE_TE, D_MODEL), lambda i, j: (j, 0)),
                  pl.BlockSpec((D_MODEL, DENSE_TE), lambda i, j: (0, j)),
                  side2, side2, side, side],
        out_specs=pl.BlockSpec((D_MODEL, DENSE_TM), lambda i, j: (0, i)),
        out_shape=jax.ShapeDtypeStruct((D_MODEL, t), F32),
        scratch_shapes=[pltpu.VMEM((DENSE_TE, DENSE_TM), BF16)],
        compiler_params=_params("parallel", "arbitrary"),
        name="dense",
    )(x1b, u_bf, vt_bf, r2, e2, e1, c1)


def _final_kernel(pt_ref, x1_ref, p_ref, g_ref, b_ref, wg_ref, wp_ref, out_ref):
    peer = pt_ref[...].T
    x2 = _layer_norm(ALPHA * x1_ref[...] + peer, g_ref[...], b_ref[...])
    gate = jax.nn.sigmoid(jnp.dot(x2.astype(BF16), wg_ref[...], preferred_element_type=F32))
    proj = jnp.dot(p_ref[...].astype(BF16), wp_ref[...], preferred_element_type=F32)
    out_ref[...] = x2 + gate * proj


def _final(peer_t, x1, p2d, ln_g, ln_b, wg, wp):
    t = x1.shape[0]

    def full(shape):
        return pl.BlockSpec(shape, lambda i: (0,) * len(shape))

    row_blk = pl.BlockSpec((FINAL_TM, D_MODEL), lambda i: (i, 0))
    return pl.pallas_call(
        _final_kernel,
        grid=(t // FINAL_TM,),
        in_specs=[pl.BlockSpec((D_MODEL, FINAL_TM), lambda i: (0, i)), row_blk,
                  pl.BlockSpec((FINAL_TM, PLE_DIM), lambda i: (i, 0)),
                  full((1, D_MODEL)), full((1, D_MODEL)),
                  full((D_MODEL, D_MODEL)), full((PLE_DIM, D_MODEL))],
        out_specs=row_blk,
        out_shape=jax.ShapeDtypeStruct((t, D_MODEL), F32),
        compiler_params=_params("parallel"),
        name="final",
    )(peer_t, x1, p2d, ln_g, ln_b, wg, wp)


def kernel(x, p, w_in, lb_param, hg_norm_g, conv_w, conv_b, w_branch_a, w_branch_b, w_out, ln1_g, ln1_b,
           pk_w_q, pk_sub_keys, pk_u, pk_v, ln2_g, ln2_b, ple_w_gate, ple_w_proj):
    batch, seq, d = x.shape
    assert d == D_MODEL and w_in.shape[0] == DEPTH and lb_param.shape[0] == DEPTH + 1
    assert seq % HGRN_TS == 0 and seq % MIX_TM == 0
    t = batch * seq
    x2d = x.reshape(t, d)

    z3 = _proj(x2d.astype(BF16), w_in[0].astype(BF16))
    o_g = _hgrn(z3, lb_param, hg_norm_g, batch, seq)
    x1, x1b = _mix(o_g, z3, x2d, conv_w[0], conv_b, w_branch_a[0].astype(BF16), w_branch_b[0].astype(BF16),
                   w_out[0].astype(BF16), ln1_g, ln1_b, seq)

    wq_t = pk_w_q[0].T.astype(BF16)
    keys = pk_sub_keys[0].reshape(PK_HEADS * 2, N_KEYS, N_KEYS).astype(BF16)
    r2, e2, e1, c1 = _topk(x1b, wq_t, keys)
    peer_t = _dense(x1b, pk_u[0].astype(BF16), pk_v[0].T.astype(BF16), r2, e2, e1, c1)

    out = _final(peer_t, x1, p[0].reshape(t, PLE_DIM), ln2_g, ln2_b,
                 ple_w_gate[0].astype(BF16), ple_w_proj[0].astype(BF16))
    return out.reshape(batch, seq, d)
```

```python
import functools
import math

import jax
import jax.numpy as jnp
from jax import lax
from jax.experimental import pallas as pl
from jax.experimental.pallas import tpu as pltpu

D_MODEL = 1024
HEADS = 8
HEAD_DIM = 128
CHUNK = 64
LANES = 128
BF16_ROWS = 16
N_KEYS = 128
PK_HEADS = 8
PK_TOPK = 16
N_EXPERTS = N_KEYS * N_KEYS
PLE_DIM = 256
LN_EPS = 1e-5
RMS_EPS = 1e-6
DEPTH = 1
ALPHA = (2.0 * DEPTH) ** 0.25
VMEM_LIMIT_BYTES = 56 * 1024 * 1024

F32 = jnp.float32
BF16 = jnp.bfloat16
NT_DIMS = (((1,), (1,)), ((), ()))
TN_DIMS = (((0,), (0,)), ((), ()))

HGRN_TS, HGRN_SUB = 512, 256
HGRN_HP = 4
MIX_TM = 256
TOPK_TM = 256
DENSE_TM, DENSE_TE = 512, 2048
FINAL_TM = 256


def _params(*sem):
    return pltpu.CompilerParams(dimension_semantics=sem, vmem_limit_bytes=VMEM_LIMIT_BYTES)


def _layer_norm(y, g, b):
    mu = jnp.mean(y, axis=-1, keepdims=True)
    yc = y - mu
    var = jnp.mean(yc * yc, axis=-1, keepdims=True)
    return yc * lax.rsqrt(var + LN_EPS) * g + b


def _hgrn_kernel(x_ref, wq_ref, wf_ref, wi_ref, wg_ref, lbp_ref, g_ref, o_ref, st_ref):
    @pl.when(pl.program_id(2) == 0)
    def _():
        st_ref[...] = jnp.zeros_like(st_ref)

    x = x_ref[...]
    zq = jnp.dot(x, wq_ref[...], preferred_element_type=F32)
    zf = jnp.dot(x, wf_ref[...], preferred_element_type=F32)
    zi = jnp.dot(x, wi_ref[...], preferred_element_type=F32)
    zg = jnp.dot(x, wg_ref[...], preferred_element_type=F32)

    r = lax.broadcasted_iota(jnp.int32, (HGRN_SUB, HGRN_SUB), 0)
    c = lax.broadcasted_iota(jnp.int32, (HGRN_SUB, HGRN_SUB), 1)
    causal = ((r // CHUNK) == (c // CHUNK)) & (c <= r)
    tril = causal.astype(BF16)
    scale = HEAD_DIM ** -0.5

    n_sub = HGRN_TS // HGRN_SUB
    n_chunk = HGRN_SUB // CHUNK
    items = [(hp, s) for s in range(n_sub) for hp in range(HGRN_HP)]
    k_all, bcum_all, qe_all, v_all, o_all = {}, {}, {}, {}, {}
    for hp, s in items:
        cols = slice(hp * LANES, (hp + 1) * LANES)
        rows = slice(s * HGRN_SUB, (s + 1) * HGRN_SUB)
        lbp = lbp_ref[:, cols]
        lbe = jnp.exp(lbp - jnp.max(lbp, axis=0, keepdims=True))
        lb = lbe[0:1] / jnp.sum(lbe, axis=0, keepdims=True)
        fz = zf[rows, cols]
        logf = jnp.log(lb + (1.0 - lb) * jax.nn.sigmoid(fz))
        k_all[hp, s] = (1.0 - lb) * jax.nn.sigmoid(-fz)
        p0 = logf.astype(BF16)
        p1 = (logf - p0.astype(F32)).astype(BF16)
        bcum_all[hp, s] = (jnp.dot(tril, p0, preferred_element_type=F32)
                           + jnp.dot(tril, p1, preferred_element_type=F32))
    for hp, s in items:
        cols = slice(hp * LANES, (hp + 1) * LANES)
        rows = slice(s * HGRN_SUB, (s + 1) * HGRN_SUB)
        bcum = bcum_all[hp, s]
        qe = (zq[rows, cols] * scale * jnp.exp(bcum)).astype(BF16)
        ke = (k_all[hp, s] * jnp.exp(-bcum)).astype(BF16)
        v_bf = zi[rows, cols].astype(BF16)
        a = lax.dot_general(qe, ke, NT_DIMS, preferred_element_type=F32)
        a = jnp.where(causal, a, 0.0).astype(BF16)
        o_all[hp, s] = jnp.dot(a, v_bf, preferred_element_type=F32)
        qe_all[hp, s], v_all[hp, s] = qe, v_bf
    outs = {item: [] for item in items}
    for s in range(n_sub):
        for ci in range(n_chunk):
            lo, hi = ci * CHUNK, (ci + 1) * CHUNK
            for hp in range(HGRN_HP):
                bcum, k, qe, v_bf = bcum_all[hp, s], k_all[hp, s], qe_all[hp, s], v_all[hp, s]
                b_last = bcum[hi - 1:hi, :]
                k_end = (k[lo:hi] * jnp.exp(b_last - bcum[lo:hi])).astype(BF16)
                st = st_ref[hp]
                o_inter = lax.dot_general(qe[lo:hi], st.astype(BF16), NT_DIMS, preferred_element_type=F32)
                ds_t = lax.dot_general(v_bf[lo:hi], k_end, TN_DIMS, preferred_element_type=F32)
                st_ref[hp] = st * jnp.exp(b_last) + ds_t
                outs[hp, s].append(o_all[hp, s][lo:hi] + o_inter)
    for hp, s in items:
        cols = slice(hp * LANES, (hp + 1) * LANES)
        rows = slice(s * HGRN_SUB, (s + 1) * HGRN_SUB)
        o = jnp.concatenate(outs[hp, s], axis=0)
        o = o * lax.rsqrt(jnp.mean(o * o, axis=-1, keepdims=True) + RMS_EPS) * g_ref[:, cols]
        og = zg[rows, cols]
        o_ref[rows, cols] = (o * (og * jax.nn.sigmoid(og))).astype(BF16)


def _hgrn(x_bf, w_bf, lb_param, norm_g, batch, seq):
    nt = seq // HGRN_TS
    nhb = HEADS // HGRN_HP
    width = HGRN_HP * LANES

    def w_cols(group):
        return pl.BlockSpec((D_MODEL, width), lambda b, hb, n: (0, group * nhb + hb))

    return pl.pallas_call(
        _hgrn_kernel,
        grid=(batch, nhb, nt),
        in_specs=[pl.BlockSpec((HGRN_TS, D_MODEL), lambda b, hb, n: (b * nt + n, 0)),
                  w_cols(0), w_cols(1), w_cols(2), w_cols(3),
                  pl.BlockSpec((DEPTH + 1, width), lambda b, hb, n: (0, hb)),
                  pl.BlockSpec((1, width), lambda b, hb, n: (0, hb))],
        out_specs=pl.BlockSpec((HGRN_TS, width), lambda b, hb, n: (b * nt + n, hb)),
        out_shape=jax.ShapeDtypeStruct((batch * seq, D_MODEL), BF16),
        scratch_shapes=[pltpu.VMEM((HGRN_HP, HEAD_DIM, HEAD_DIM), F32)],
        compiler_params=_params("parallel", "parallel", "arbitrary"),
        name="hgrn",
    )(x_bf, w_bf, w_bf, w_bf, w_bf, lb_param, norm_g)


def _mix_kernel(o_ref, xb_ref, x_ref, wm_ref, cw_ref, cbias_ref, wa_ref, wb_ref, wo_ref, g_ref, b_ref,
                x1_ref, x1b_ref, halo_ref, *, tiles_per_seq):
    @pl.when((pl.program_id(0) % tiles_per_seq) == 0)
    def _():
        halo_ref[...] = jnp.zeros_like(halo_ref)

    z = jnp.dot(xb_ref[...], wm_ref[...], preferred_element_type=F32)
    cb, cc, ch, ga, gb = (z[:, g * D_MODEL:(g + 1) * D_MODEL] for g in range(5))
    u = cc * ch
    halo = halo_ref[...]
    halo_ref[...] = u[MIX_TM - 8:, :]
    h1 = halo[7:8, :]
    h2 = halo[6:7, :]
    row = lax.broadcasted_iota(jnp.int32, (MIX_TM, D_MODEL), 0)
    u1 = jnp.where(row == 0, h1, pltpu.roll(u, 1, 0))
    u2 = jnp.where(row == 0, h2, jnp.where(row == 1, h1, pltpu.roll(u, 2, 0)))
    y = cbias_ref[...] + cw_ref[0:1, :] * u2 + cw_ref[1:2, :] * u1 + cw_ref[2:3, :] * u
    yc = (cb * y).astype(BF16)
    ta = jnp.dot(o_ref[...], wa_ref[...], preferred_element_type=F32)
    tb = jnp.dot(yc, wb_ref[...], preferred_element_type=F32)
    merged = jax.nn.sigmoid(ga) * ta + jax.nn.sigmoid(gb) * tb
    mix = jnp.dot(merged.astype(BF16), wo_ref[...], preferred_element_type=F32)
    x1 = _layer_norm(ALPHA * x_ref[...] + mix, g_ref[...], b_ref[...])
    x1_ref[...] = x1
    x1b_ref[...] = x1.astype(BF16)


def _mix(o_g, x_bf, x2d, w_mix, conv_w, conv_b, wa, wb, wo, ln_g, ln_b, seq):
    t = x2d.shape[0]

    def full(shape):
        return pl.BlockSpec(shape, lambda i: (0,) * len(shape), pipeline_mode=pl.Buffered(1))

    row_blk = pl.BlockSpec((MIX_TM, D_MODEL), lambda i: (i, 0))
    return pl.pallas_call(
        functools.partial(_mix_kernel, tiles_per_seq=seq // MIX_TM),
        grid=(t // MIX_TM,),
        in_specs=[row_blk, row_blk, row_blk, full(w_mix.shape),
                  full((3, D_MODEL)), full((1, D_MODEL)),
                  full((D_MODEL, D_MODEL)), full((D_MODEL, D_MODEL)), full((D_MODEL, D_MODEL)),
                  full((1, D_MODEL)), full((1, D_MODEL))],
        out_specs=[row_blk, row_blk],
        out_shape=[jax.ShapeDtypeStruct((t, D_MODEL), F32), jax.ShapeDtypeStruct((t, D_MODEL), BF16)],
        scratch_shapes=[pltpu.VMEM((8, D_MODEL), F32)],
        compiler_params=_params("arbitrary"),
        name="mix",
    )(o_g, x_bf, x2d, w_mix, conv_w, conv_b, wa, wb, wo, ln_g, ln_b)


def _top_values(s, k, with_rank=False):
    rows = []
    rank = jnp.full(s.shape, float(k), F32) if with_rank else None
    for r in range(k):
        m = jnp.max(s, axis=0, keepdims=True)
        rows.append(m)
        hit = s == m
        if with_rank:
            rank = jnp.where(hit, float(r), rank)
        s = jnp.where(hit, -jnp.inf, s)
    return (rows, rank) if with_rank else rows


def _count_reaching(s1, b, tau):
    assert len(b) == 16

    def reach(b_row):
        return (s1 + b_row) >= tau

    def pick(t, v):
        return jnp.where(t, v, 0.0)

    t8 = reach(b[7])
    t4 = reach(jnp.where(t8, b[11], b[3]))
    t2 = reach(jnp.where(t8, jnp.where(t4, b[13], b[9]), jnp.where(t4, b[5], b[1])))
    upper = jnp.where(t4, jnp.where(t2, b[14], b[12]), jnp.where(t2, b[10], b[8]))
    lower = jnp.where(t4, jnp.where(t2, b[6], b[4]), jnp.where(t2, b[2], b[0]))
    t1 = reach(jnp.where(t8, upper, lower))
    t16 = reach(b[15])
    return pick(t8, 8.0) + pick(t4, 4.0) + pick(t2, 2.0) + pick(t1, 1.0) + pick(t16, 1.0)


def _topk_kernel(x_ref, wq_ref, keys_ref, r2_ref, e2_ref, e1_ref, c1_ref):
    q_t = lax.dot_general(wq_ref[...], x_ref[...], NT_DIMS, preferred_element_type=F32)
    for h in range(PK_HEADS):
        q1 = q_t[(2 * h) * LANES:(2 * h + 1) * LANES].astype(BF16)
        q2 = q_t[(2 * h + 1) * LANES:(2 * h + 2) * LANES].astype(BF16)
        s1 = jnp.dot(keys_ref[2 * h], q1, preferred_element_type=F32)
        s2 = jnp.dot(keys_ref[2 * h + 1], q2, preferred_element_type=F32)
        a = _top_values(s1, PK_TOPK)
        b, rank2 = _top_values(s2, PK_TOPK, with_rank=True)
        a_all = jnp.concatenate(a, axis=0)
        b_all = jnp.concatenate(b, axis=0)
        half = PK_TOPK // 2
        cand = jnp.concatenate([a[0] + b_all] + [a[r1] + b_all[:half] for r1 in range(1, half)]
                               + [a_all[half:] + b[0]], axis=0)
        tau = _top_values(cand, PK_TOPK)[-1]
        top = a[0] + b[0]
        z = jnp.sum(jnp.where(cand >= tau, jnp.exp(cand - top), 0.0), axis=0, keepdims=True)
        count1 = _count_reaching(s1, b, tau)
        r2_ref[h * N_KEYS:(h + 1) * N_KEYS, :] = rank2.astype(BF16)
        e2_ref[h * N_KEYS:(h + 1) * N_KEYS, :] = jnp.exp(s2 - b[0]).astype(BF16)
        e1_ref[h] = jnp.exp(s1 - a[0]) * (0.5 / z)
        c1_ref[h] = count1


def _topk(x1b, wq_t, keys):
    t = x1b.shape[0]
    side = pl.BlockSpec((PK_HEADS, N_KEYS, TOPK_TM), lambda i: (0, 0, i))
    f32_side = jax.ShapeDtypeStruct((PK_HEADS, N_KEYS, t), F32)
    side2 = pl.BlockSpec((PK_HEADS * N_KEYS, TOPK_TM), lambda i: (0, i))
    bf16_side = jax.ShapeDtypeStruct((PK_HEADS * N_KEYS, t), BF16)
    return pl.pallas_call(
        _topk_kernel,
        grid=(t // TOPK_TM,),
        in_specs=[pl.BlockSpec((TOPK_TM, D_MODEL), lambda i: (i, 0)),
                  pl.BlockSpec(wq_t.shape, lambda i: (0, 0)),
                  pl.BlockSpec(keys.shape, lambda i: (0, 0, 0))],
        out_specs=[side2, side2, side, side],
        out_shape=[bf16_side, bf16_side, f32_side, f32_side],
        compiler_params=_params("parallel"),
        name="topk",
    )(x1b, wq_t, keys)


def _bcast_rows_bf16(row):
    tile = jnp.broadcast_to(row, (BF16_ROWS, LANES)).astype(BF16)
    return jnp.broadcast_to(tile[None], (N_KEYS // BF16_ROWS, BF16_ROWS, LANES)).reshape(N_KEYS, LANES)


def _dense_kernel(x_ref, u_ref, vt_ref, r2_ref, e2_ref, e1_ref, c1_ref, o_ref):
    j = pl.program_id(1)

    @pl.when(j == 0)
    def _():
        o_ref[...] = jnp.zeros_like(o_ref)

    n_key1 = DENSE_TE // N_KEYS
    e1_rows = [[e1_ref[h, pl.ds(j * n_key1 + a, 1), :] for h in range(PK_HEADS)] for a in range(n_key1)]
    c1_rows = [[c1_ref[h, pl.ds(j * n_key1 + a, 1), :] for h in range(PK_HEADS)] for a in range(n_key1)]
    h_t = lax.dot_general(u_ref[...], x_ref[...], NT_DIMS, preferred_element_type=F32)
    act = (h_t * (1.0 + lax.erf(h_t * (1.0 / math.sqrt(2.0))))).astype(BF16)
    cols = []
    for tc in range(DENSE_TM // LANES):
        lanes = slice(tc * LANES, (tc + 1) * LANES)
        parts = []
        for a in range(n_key1):
            w = jnp.zeros((N_KEYS, LANES), BF16)
            for h in range(PK_HEADS):
                e1 = _bcast_rows_bf16(e1_rows[a][h][:, lanes])
                c1 = _bcast_rows_bf16(c1_rows[a][h][:, lanes])
                keys2 = slice(h * N_KEYS, (h + 1) * N_KEYS)
                w = w + jnp.where(c1 > r2_ref[keys2, lanes], e1 * e2_ref[keys2, lanes], 0.0)
            parts.append(w * act[a * N_KEYS:(a + 1) * N_KEYS, lanes])
        cols.append(jnp.concatenate(parts, axis=0))
    a_t = jnp.concatenate(cols, axis=1)
    o_ref[...] += jnp.dot(vt_ref[...], a_t, preferred_element_type=F32)


def _dense(x1b, u_bf, vt_bf, r2, e2, e1, c1):
    t = x1b.shape[0]
    side = pl.BlockSpec((PK_HEADS, N_KEYS, DENSE_TM), lambda i, j: (0, 0, i))
    side2 = pl.BlockSpec((PK_HEADS * N_KEYS, DENSE_TM), lambda i, j: (0, i))
    return pl.pallas_call(
        _dense_kernel,
        grid=(t // DENSE_TM, N_EXPERTS // DENSE_TE),
        in_specs=[pl.BlockSpec((DENSE_TM, D_MODEL), lambda i, j: (i, 0)),
                  pl.BlockSpec((DENSE_TE, D_MODEL), lambda i, j: (j, 0)),
                  pl.BlockSpec((D_MODEL, DENSE_TE), lambda i, j: (0, j)),
                  side2, side2, side, side],
        out_specs=pl.BlockSpec((D_MODEL, DENSE_TM), lambda i, j: (0, i)),
        out_shape=jax.ShapeDtypeStruct((D_MODEL, t), F32),
        compiler_params=_params("parallel", "arbitrary"),
        name="dense",
    )(x1b, u_bf, vt_bf, r2, e2, e1, c1)


def _final_kernel(pt_ref, x1_ref, p_ref, g_ref, b_ref, wg_ref, wp_ref, out_ref):
    peer = pt_ref[...].T
    x2 = _layer_norm(ALPHA * x1_ref[...] + peer, g_ref[...], b_ref[...])
    gate = jax.nn.sigmoid(jnp.dot(x2.astype(BF16), wg_ref[...], preferred_element_type=F32))
    proj = jnp.dot(p_ref[...].astype(BF16), wp_ref[...], preferred_element_type=F32)
    out_ref[...] = x2 + gate * proj


def _final(peer_t, x1, p2d, ln_g, ln_b, wg, wp):
    t = x1.shape[0]

    def full(shape):
        return pl.BlockSpec(shape, lambda i: (0,) * len(shape))

    row_blk = pl.BlockSpec((FINAL_TM, D_MODEL), lambda i: (i, 0))
    return pl.pallas_call(
        _final_kernel,
        grid=(t // FINAL_TM,),
        in_specs=[pl.BlockSpec((D_MODEL, FINAL_TM), lambda i: (0, i)), row_blk,
                  pl.BlockSpec((FINAL_TM, PLE_DIM), lambda i: (i, 0)),
                  full((1, D_MODEL)), full((1, D_MODEL)),
                  full((D_MODEL, D_MODEL)), full((PLE_DIM, D_MODEL))],
        out_specs=row_blk,
        out_shape=jax.ShapeDtypeStruct((t, D_MODEL), F32),
        compiler_params=_params("parallel"),
        name="final",
    )(peer_t, x1, p2d, ln_g, ln_b, wg, wp)


def kernel(x, p, w_in, lb_param, hg_norm_g, conv_w, conv_b, w_branch_a, w_branch_b, w_out, ln1_g, ln1_b,
           pk_w_q, pk_sub_keys, pk_u, pk_v, ln2_g, ln2_b, ple_w_gate, ple_w_proj):
    batch, seq, d = x.shape
    assert d == D_MODEL and w_in.shape[0] == DEPTH and lb_param.shape[0] == DEPTH + 1
    assert seq % HGRN_TS == 0 and seq % MIX_TM == 0
    t = batch * seq
    x2d = x.reshape(t, d)

    x_bf = x2d.astype(BF16)
    w_bf = w_in[0].astype(BF16)
    hg_w = HEADS * HEAD_DIM * 4
    o_g = _hgrn(x_bf, w_bf[:, :hg_w], lb_param, hg_norm_g, batch, seq)
    x1, x1b = _mix(o_g, x_bf, x2d, w_bf[:, hg_w:], conv_w[0], conv_b, w_branch_a[0].astype(BF16),
                   w_branch_b[0].astype(BF16), w_out[0].astype(BF16), ln1_g, ln1_b, seq)

    wq_t = pk_w_q[0].T.astype(BF16)
    keys = pk_sub_keys[0].reshape(PK_HEADS * 2, N_KEYS, N_KEYS).astype(BF16)
    r2, e2, e1, c1 = _topk(x1b, wq_t, keys)
    peer_t = _dense(x1b, pk_u[0].astype(BF16), pk_v[0].T.astype(BF16), r2, e2, e1, c1)

    out = _final(peer_t, x1, p[0].reshape(t, PLE_DIM), ln2_g, ln2_b,
                 ple_w_gate[0].astype(BF16), ple_w_proj[0].astype(BF16))
    return out.reshape(batch, seq, d)
```

```python
import functools
import math

import jax
import jax.numpy as jnp
from jax import lax
from jax.experimental import pallas as pl
from jax.experimental.pallas import tpu as pltpu

D_MODEL = 1024
HEADS = 8
HEAD_DIM = 128
CHUNK = 64
LANES = 128
BF16_ROWS = 16
N_KEYS = 128
PK_HEADS = 8
PK_TOPK = 16
N_EXPERTS = N_KEYS * N_KEYS
PLE_DIM = 256
LN_EPS = 1e-5
RMS_EPS = 1e-6
DEPTH = 1
ALPHA = (2.0 * DEPTH) ** 0.25
VMEM_LIMIT_BYTES = 56 * 1024 * 1024

F32 = jnp.float32
BF16 = jnp.bfloat16
NT_DIMS = (((1,), (1,)), ((), ()))
TN_DIMS = (((0,), (0,)), ((), ()))

HGRN_TS, HGRN_SUB = 512, 256
HGRN_HP = 4
MIX_TM = 256
TOPK_TM = 256
DENSE_TM, DENSE_TE = 512, 2048
DENSE_TH = DENSE_TE // 2
DENSE_TQ = 512
FINAL_TM = 256


def _params(*sem):
    return pltpu.CompilerParams(dimension_semantics=sem, vmem_limit_bytes=VMEM_LIMIT_BYTES)


def _layer_norm(y, g, b):
    mu = jnp.mean(y, axis=-1, keepdims=True)
    yc = y - mu
    var = jnp.mean(yc * yc, axis=-1, keepdims=True)
    return yc * lax.rsqrt(var + LN_EPS) * g + b


def _hgrn_kernel(x_ref, wq_ref, wf_ref, wi_ref, wg_ref, lbp_ref, g_ref, o_ref, st_ref):
    @pl.when(pl.program_id(2) == 0)
    def _():
        st_ref[...] = jnp.zeros_like(st_ref)

    x = x_ref[...].astype(BF16)
    zq = jnp.dot(x, wq_ref[...], preferred_element_type=F32)
    zf = jnp.dot(x, wf_ref[...], preferred_element_type=F32)
    zi = jnp.dot(x, wi_ref[...], preferred_element_type=F32)
    zg = jnp.dot(x, wg_ref[...], preferred_element_type=F32)

    r = lax.broadcasted_iota(jnp.int32, (HGRN_SUB, HGRN_SUB), 0)
    c = lax.broadcasted_iota(jnp.int32, (HGRN_SUB, HGRN_SUB), 1)
    causal = ((r // CHUNK) == (c // CHUNK)) & (c <= r)
    tril = causal.astype(BF16)
    scale = HEAD_DIM ** -0.5

    n_sub = HGRN_TS // HGRN_SUB
    n_chunk = HGRN_SUB // CHUNK
    items = [(hp, s) for s in range(n_sub) for hp in range(HGRN_HP)]
    k_all, bcum_all, qe_all, v_all, o_all = {}, {}, {}, {}, {}
    for hp, s in items:
        cols = slice(hp * LANES, (hp + 1) * LANES)
        rows = slice(s * HGRN_SUB, (s + 1) * HGRN_SUB)
        lbp = lbp_ref[:, cols]
        lbe = jnp.exp(lbp - jnp.max(lbp, axis=0, keepdims=True))
        lb = lbe[0:1] / jnp.sum(lbe, axis=0, keepdims=True)
        fz = zf[rows, cols]
        logf = jnp.log(lb + (1.0 - lb) * jax.nn.sigmoid(fz))
        k_all[hp, s] = (1.0 - lb) * jax.nn.sigmoid(-fz)
        p0 = logf.astype(BF16)
        p1 = (logf - p0.astype(F32)).astype(BF16)
        bcum_all[hp, s] = (jnp.dot(tril, p0, preferred_element_type=F32)
                           + jnp.dot(tril, p1, preferred_element_type=F32))
    for hp, s in items:
        cols = slice(hp * LANES, (hp + 1) * LANES)
        rows = slice(s * HGRN_SUB, (s + 1) * HGRN_SUB)
        bcum = bcum_all[hp, s]
        qe = (zq[rows, cols] * scale * jnp.exp(bcum)).astype(BF16)
        ke = (k_all[hp, s] * jnp.exp(-bcum)).astype(BF16)
        v_bf = zi[rows, cols].astype(BF16)
        a = lax.dot_general(qe, ke, NT_DIMS, preferred_element_type=F32)
        a = jnp.where(causal, a, 0.0).astype(BF16)
        o_all[hp, s] = jnp.dot(a, v_bf, preferred_element_type=F32)
        qe_all[hp, s], v_all[hp, s] = qe, v_bf
    outs = {item: [] for item in items}
    for s in range(n_sub):
        for ci in range(n_chunk):
            lo, hi = ci * CHUNK, (ci + 1) * CHUNK
            for hp in range(HGRN_HP):
                bcum, k, qe, v_bf = bcum_all[hp, s], k_all[hp, s], qe_all[hp, s], v_all[hp, s]
                b_last = bcum[hi - 1:hi, :]
                k_end = (k[lo:hi] * jnp.exp(b_last - bcum[lo:hi])).astype(BF16)
                st = st_ref[hp]
                o_inter = lax.dot_general(qe[lo:hi], st.astype(BF16), NT_DIMS, preferred_element_type=F32)
                ds_t = lax.dot_general(v_bf[lo:hi], k_end, TN_DIMS, preferred_element_type=F32)
                st_ref[hp] = st * jnp.exp(b_last) + ds_t
                outs[hp, s].append(o_all[hp, s][lo:hi] + o_inter)
    for hp, s in items:
        cols = slice(hp * LANES, (hp + 1) * LANES)
        rows = slice(s * HGRN_SUB, (s + 1) * HGRN_SUB)
        o = jnp.concatenate(outs[hp, s], axis=0)
        o = o * lax.rsqrt(jnp.mean(o * o, axis=-1, keepdims=True) + RMS_EPS) * g_ref[:, cols]
        og = zg[rows, cols]
        o_ref[rows, cols] = (o * (og * jax.nn.sigmoid(og))).astype(BF16)


def _hgrn(x2d, w_bf, lb_param, norm_g, batch, seq):
    nt = seq // HGRN_TS
    nhb = HEADS // HGRN_HP
    width = HGRN_HP * LANES

    def w_cols(group):
        return pl.BlockSpec((D_MODEL, width), lambda b, hb, n: (0, group * nhb + hb))

    return pl.pallas_call(
        _hgrn_kernel,
        grid=(batch, nhb, nt),
        in_specs=[pl.BlockSpec((HGRN_TS, D_MODEL), lambda b, hb, n: (b * nt + n, 0)),
                  w_cols(0), w_cols(1), w_cols(2), w_cols(3),
                  pl.BlockSpec((DEPTH + 1, width), lambda b, hb, n: (0, hb)),
                  pl.BlockSpec((1, width), lambda b, hb, n: (0, hb))],
        out_specs=pl.BlockSpec((HGRN_TS, width), lambda b, hb, n: (b * nt + n, hb)),
        out_shape=jax.ShapeDtypeStruct((batch * seq, D_MODEL), BF16),
        scratch_shapes=[pltpu.VMEM((HGRN_HP, HEAD_DIM, HEAD_DIM), F32)],
        compiler_params=_params("parallel", "parallel", "arbitrary"),
        name="hgrn",
    )(x2d, w_bf, w_bf, w_bf, w_bf, lb_param, norm_g)


def _mix_kernel(o_ref, x_ref, wcb_ref, wcc_ref, wch_ref, wga_ref, wgb_ref, cw_ref, cbias_ref, wa_ref, wb_ref, wo_ref,
                g_ref, b_ref, x1_ref, x1b_ref, halo_ref, *, tiles_per_seq):
    @pl.when((pl.program_id(0) % tiles_per_seq) == 0)
    def _():
        halo_ref[...] = jnp.zeros_like(halo_ref)

    x = x_ref[...]
    xb = x.astype(BF16)
    cb, cc, ch, ga, gb = (jnp.dot(xb, w_ref[...], preferred_element_type=F32)
                          for w_ref in (wcb_ref, wcc_ref, wch_ref, wga_ref, wgb_ref))
    u = cc * ch
    halo = halo_ref[...]
    halo_ref[...] = u[MIX_TM - 8:, :]
    h1 = halo[7:8, :]
    h2 = halo[6:7, :]
    row = lax.broadcasted_iota(jnp.int32, (MIX_TM, D_MODEL), 0)
    u1 = jnp.where(row == 0, h1, pltpu.roll(u, 1, 0))
    u2 = jnp.where(row == 0, h2, jnp.where(row == 1, h1, pltpu.roll(u, 2, 0)))
    y = cbias_ref[...] + cw_ref[0:1, :] * u2 + cw_ref[1:2, :] * u1 + cw_ref[2:3, :] * u
    yc = (cb * y).astype(BF16)
    ta = jnp.dot(o_ref[...], wa_ref[...], preferred_element_type=F32)
    tb = jnp.dot(yc, wb_ref[...], preferred_element_type=F32)
    merged = jax.nn.sigmoid(ga) * ta + jax.nn.sigmoid(gb) * tb
    mix = jnp.dot(merged.astype(BF16), wo_ref[...], preferred_element_type=F32)
    x1 = _layer_norm(ALPHA * x + mix, g_ref[...], b_ref[...])
    x1_ref[...] = x1
    x1b_ref[...] = x1.astype(BF16)


def _mix(o_g, x2d, w_bf, conv_w, conv_b, wa, wb, wo, ln_g, ln_b, seq):
    t = x2d.shape[0]

    def full(shape):
        return pl.BlockSpec(shape, lambda i: (0,) * len(shape), pipeline_mode=pl.Buffered(1))

    def w_cols(group):
        return pl.BlockSpec((D_MODEL, D_MODEL), lambda i: (0, group), pipeline_mode=pl.Buffered(1))

    row_blk = pl.BlockSpec((MIX_TM, D_MODEL), lambda i: (i, 0))
    return pl.pallas_call(
        functools.partial(_mix_kernel, tiles_per_seq=seq // MIX_TM),
        grid=(t // MIX_TM,),
        in_specs=[row_blk, row_blk, w_cols(4), w_cols(5), w_cols(6), w_cols(7), w_cols(8),
                  full((3, D_MODEL)), full((1, D_MODEL)),
                  full((D_MODEL, D_MODEL)), full((D_MODEL, D_MODEL)), full((D_MODEL, D_MODEL)),
                  full((1, D_MODEL)), full((1, D_MODEL))],
        out_specs=[row_blk, row_blk],
        out_shape=[jax.ShapeDtypeStruct((t, D_MODEL), F32), jax.ShapeDtypeStruct((t, D_MODEL), BF16)],
        scratch_shapes=[pltpu.VMEM((8, D_MODEL), F32)],
        compiler_params=_params("arbitrary"),
        name="mix",
    )(o_g, x2d, w_bf, w_bf, w_bf, w_bf, w_bf, conv_w, conv_b, wa, wb, wo, ln_g, ln_b)


def _kth_largest(s, k):
    for _ in range(k - 1):
        s = jnp.where(s == jnp.max(s, axis=0, keepdims=True), -jnp.inf, s)
    return jnp.max(s, axis=0, keepdims=True)


def _oddeven_merge(lo, hi, r):
    step = r * 2
    if step < hi - lo:
        yield from _oddeven_merge(lo, hi, step)
        yield from _oddeven_merge(lo + r, hi, step)
        yield from [(i, i + r) for i in range(lo + r, hi - r, step)]
    else:
        yield (lo, lo + r)


def _oddeven_merge_sort(lo, hi):
    if hi - lo >= 1:
        mid = lo + (hi - lo) // 2
        yield from _oddeven_merge_sort(lo, mid)
        yield from _oddeven_merge_sort(mid + 1, hi)
        yield from _oddeven_merge(lo, hi, 1)


def _exchange(v, i, j):
    v[i], v[j] = jnp.maximum(v[i], v[j]), jnp.minimum(v[i], v[j])


def _top16_sorted(s):
    n = PK_TOPK
    v = [s[k * 8:(k + 1) * 8, :] for k in range(n)]
    for i, j in _oddeven_merge_sort(0, n - 1):
        _exchange(v, i, j)
    for shift in (4, 2, 1):
        rolled = [pltpu.roll(vk, shift, 0) for vk in v]
        v = [jnp.maximum(v[k], rolled[n - 1 - k]) for k in range(n)]
        d = n // 2
        while d:
            for k in range(n):
                if not k & d:
                    _exchange(v, k, k + d)
            d //= 2
    return v


def _count_true(test, b):
    assert len(b) == 16

    def pick(t, v):
        return jnp.where(t, v, 0.0)

    t8 = test(b[7])
    t4 = test(jnp.where(t8, b[11], b[3]))
    t2 = test(jnp.where(t8, jnp.where(t4, b[13], b[9]), jnp.where(t4, b[5], b[1])))
    upper = jnp.where(t4, jnp.where(t2, b[14], b[12]), jnp.where(t2, b[10], b[8]))
    lower = jnp.where(t4, jnp.where(t2, b[6], b[4]), jnp.where(t2, b[2], b[0]))
    t1 = test(jnp.where(t8, upper, lower))
    t16 = test(b[15])
    return pick(t8, 8.0) + pick(t4, 4.0) + pick(t2, 2.0) + pick(t1, 1.0) + pick(t16, 1.0)


def _gather_sublanes(reps):
    sub = lax.broadcasted_iota(jnp.int32, reps[0].shape, 0)
    out = reps[0]
    for r in range(1, len(reps)):
        out = jnp.where(sub == r, reps[r], out)
    return out


def _topk_kernel(x_ref, wq_ref, keys_ref, r2_ref, e2_ref, e1_ref, c1_ref):
    q_t = lax.dot_general(wq_ref[...], x_ref[...], NT_DIMS, preferred_element_type=F32)
    for h in range(PK_HEADS):
        q1 = q_t[(2 * h) * LANES:(2 * h + 1) * LANES].astype(BF16)
        q2 = q_t[(2 * h + 1) * LANES:(2 * h + 2) * LANES].astype(BF16)
        s1 = jnp.dot(keys_ref[2 * h], q1, preferred_element_type=F32)
        s2 = jnp.dot(keys_ref[2 * h + 1], q2, preferred_element_type=F32)
        a_rep = _top16_sorted(s1)
        b_rep = _top16_sorted(s2)
        a = [v[0:1, :] for v in a_rep]
        b = [v[0:1, :] for v in b_rep]
        half = PK_TOPK // 2
        b_lo, b_hi = _gather_sublanes(b_rep[:half]), _gather_sublanes(b_rep[half:])
        a_hi = _gather_sublanes(a_rep[half:])
        cand = jnp.concatenate([a_rep[0] + b_lo, a_rep[0] + b_hi] + [a_rep[r1] + b_lo for r1 in range(1, half)]
                               + [a_hi + b_rep[0]], axis=0)
        tau = _kth_largest(cand, PK_TOPK)
        top = a[0] + b[0]
        z = jnp.sum(jnp.where(cand >= tau, jnp.exp(cand - top), 0.0), axis=0, keepdims=True)
        count1 = _count_true(lambda b_row: (s1 + b_row) >= tau, b)
        rank2 = _count_true(lambda b_row: b_row > s2, b)
        r2_ref[h * N_KEYS:(h + 1) * N_KEYS, :] = rank2.astype(BF16)
        e2_ref[h * N_KEYS:(h + 1) * N_KEYS, :] = jnp.exp(s2 - b[0]).astype(BF16)
        e1_ref[h] = jnp.exp(s1 - a[0]) * (0.5 / z)
        c1_ref[h] = count1


def _topk(x1b, wq_t, keys):
    t = x1b.shape[0]
    side = pl.BlockSpec((PK_HEADS, N_KEYS, TOPK_TM), lambda i: (0, 0, i))
    f32_side = jax.ShapeDtypeStruct((PK_HEADS, N_KEYS, t), F32)
    side2 = pl.BlockSpec((PK_HEADS * N_KEYS, TOPK_TM), lambda i: (0, i))
    bf16_side = jax.ShapeDtypeStruct((PK_HEADS * N_KEYS, t), BF16)
    return pl.pallas_call(
        _topk_kernel,
        grid=(t // TOPK_TM,),
        in_specs=[pl.BlockSpec((TOPK_TM, D_MODEL), lambda i: (i, 0)),
                  pl.BlockSpec(wq_t.shape, lambda i: (0, 0)),
                  pl.BlockSpec(keys.shape, lambda i: (0, 0, 0))],
        out_specs=[side2, side2, side, side],
        out_shape=[bf16_side, bf16_side, f32_side, f32_side],
        compiler_params=_params("parallel"),
        name="topk",
    )(x1b, wq_t, keys)


def _bcast_rows_bf16(row):
    tile = jnp.broadcast_to(row, (BF16_ROWS, LANES)).astype(BF16)
    return jnp.broadcast_to(tile[None], (N_KEYS // BF16_ROWS, BF16_ROWS, LANES)).reshape(N_KEYS, LANES)


def _expert_act(u, x):
    h_t = lax.dot_general(u, x, NT_DIMS, preferred_element_type=F32)
    return (h_t * (1.0 + lax.erf(h_t * (1.0 / math.sqrt(2.0))))).astype(BF16)


def _act_first_kernel(x_ref, u_ref, act_ref):
    act_ref[...] = _expert_act(u_ref[...], x_ref[...])


def _act_first(x1b, u_bf):
    return pl.pallas_call(
        _act_first_kernel,
        grid=(1,),
        in_specs=[pl.BlockSpec((DENSE_TM, D_MODEL), lambda i: (0, 0)),
                  pl.BlockSpec((DENSE_TH, D_MODEL), lambda i: (0, 0))],
        out_specs=pl.BlockSpec((DENSE_TH, DENSE_TM), lambda i: (0, 0)),
        out_shape=jax.ShapeDtypeStruct((DENSE_TH, DENSE_TM), BF16),
        compiler_params=_params("arbitrary"),
        name="act_first",
    )(x1b, u_bf)


def _dense_kernel(xc_ref, xn_ref, u2_ref, un_ref, vt_ref, r2_ref, e2_ref, e1_ref, c1_ref, act0_ref, o_ref,
                  act_a, act_b):
    i, j = pl.program_id(0), pl.program_id(1)

    @pl.when((i == 0) & (j == 0))
    def _():
        act_a[...] = act0_ref[...]

    @pl.when(j == 0)
    def _():
        o_ref[...] = jnp.zeros_like(o_ref)

    n_key1 = DENSE_TQ // N_KEYS
    n_qq = DENSE_TH // DENSE_TQ
    acc = None
    for half, (src, dst, u_nx, x_nx) in enumerate(((act_a, act_b, u2_ref, xc_ref), (act_b, act_a, un_ref, xn_ref))):
        for qq in range(n_qq):
            q = half * n_qq + qq
            key1_0 = j * (DENSE_TE // N_KEYS) + q * n_key1
            e1_rows = [[e1_ref[h, pl.ds(key1_0 + a, 1), :] for h in range(PK_HEADS)] for a in range(n_key1)]
            c1_rows = [[c1_ref[h, pl.ds(key1_0 + a, 1), :] for h in range(PK_HEADS)] for a in range(n_key1)]
            cols = []
            for tc in range(DENSE_TM // LANES):
                lanes = slice(tc * LANES, (tc + 1) * LANES)
                parts = []
                for a in range(n_key1):
                    w = jnp.zeros((N_KEYS, LANES), BF16)
                    for h in range(PK_HEADS):
                        e1 = _bcast_rows_bf16(e1_rows[a][h][:, lanes])
                        c1 = _bcast_rows_bf16(c1_rows[a][h][:, lanes])
                        keys2 = slice(h * N_KEYS, (h + 1) * N_KEYS)
                        w = w + jnp.where(c1 > r2_ref[keys2, lanes], e1 * e2_ref[keys2, lanes], 0.0)
                    row0 = qq * DENSE_TQ + a * N_KEYS
                    parts.append(w * src[row0:row0 + N_KEYS, lanes])
                cols.append(jnp.concatenate(parts, axis=0))
            a_q = jnp.concatenate(cols, axis=1)
            d = jnp.dot(vt_ref[:, q * DENSE_TQ:(q + 1) * DENSE_TQ], a_q, preferred_element_type=F32)
            acc = d if acc is None else acc + d
            rows = slice(qq * DENSE_TQ, (qq + 1) * DENSE_TQ)
            dst[rows, :] = _expert_act(u_nx[rows, :], x_nx[...])
    o_ref[...] += acc


def _dense(x1b, u_bf, vt_bf, r2, e2, e1, c1):
    t = x1b.shape[0]
    n_i, n_j = t // DENSE_TM, N_EXPERTS // DENSE_TE
    side = pl.BlockSpec((PK_HEADS, N_KEYS, DENSE_TM), lambda i, j: (0, 0, i))
    side2 = pl.BlockSpec((PK_HEADS * N_KEYS, DENSE_TM), lambda i, j: (0, i))
    act0 = _act_first(x1b, u_bf)
    return pl.pallas_call(
        _dense_kernel,
        grid=(n_i, n_j),
        in_specs=[pl.BlockSpec((DENSE_TM, D_MODEL), lambda i, j: (i, 0)),
                  pl.BlockSpec((DENSE_TM, D_MODEL), lambda i, j: (jnp.minimum(i + (j + 1) // n_j, n_i - 1), 0)),
                  pl.BlockSpec((DENSE_TH, D_MODEL), lambda i, j: (2 * j + 1, 0)),
                  pl.BlockSpec((DENSE_TH, D_MODEL), lambda i, j: (2 * ((j + 1) % n_j), 0)),
                  pl.BlockSpec((D_MODEL, DENSE_TE), lambda i, j: (0, j)),
                  side2, side2, side, side,
                  pl.BlockSpec((DENSE_TH, DENSE_TM), lambda i, j: (0, 0), pipeline_mode=pl.Buffered(1))],
        out_specs=pl.BlockSpec((D_MODEL, DENSE_TM), lambda i, j: (0, i)),
        out_shape=jax.ShapeDtypeStruct((D_MODEL, t), F32),
        scratch_shapes=[pltpu.VMEM((DENSE_TH, DENSE_TM), BF16), pltpu.VMEM((DENSE_TH, DENSE_TM), BF16)],
        compiler_params=_params("arbitrary", "arbitrary"),
        name="dense",
    )(x1b, x1b, u_bf, u_bf, vt_bf, r2, e2, e1, c1, act0)


def _final_kernel(pt_ref, x1_ref, p_ref, g_ref, b_ref, wg_ref, wp_ref, out_ref):
    peer = pt_ref[...].T
    x2 = _layer_norm(ALPHA * x1_ref[...] + peer, g_ref[...], b_ref[...])
    gate = jax.nn.sigmoid(jnp.dot(x2.astype(BF16), wg_ref[...], preferred_element_type=F32))
    proj = jnp.dot(p_ref[...].astype(BF16), wp_ref[...], preferred_element_type=F32)
    out_ref[...] = x2 + gate * proj


def _final(peer_t, x1, p2d, ln_g, ln_b, wg, wp):
    t = x1.shape[0]

    def full(shape):
        return pl.BlockSpec(shape, lambda i: (0,) * len(shape))

    row_blk = pl.BlockSpec((FINAL_TM, D_MODEL), lambda i: (i, 0))
    return pl.pallas_call(
        _final_kernel,
        grid=(t // FINAL_TM,),
        in_specs=[pl.BlockSpec((D_MODEL, FINAL_TM), lambda i: (0, i)), row_blk,
                  pl.BlockSpec((FINAL_TM, PLE_DIM), lambda i: (i, 0)),
                  full((1, D_MODEL)), full((1, D_MODEL)),
                  full((D_MODEL, D_MODEL)), full((PLE_DIM, D_MODEL))],
        out_specs=row_blk,
        out_shape=jax.ShapeDtypeStruct((t, D_MODEL), F32),
        compiler_params=_params("parallel"),
        name="final",
    )(peer_t, x1, p2d, ln_g, ln_b, wg, wp)


def kernel(x, p, w_in, lb_param, hg_norm_g, conv_w, conv_b, w_branch_a, w_branch_b, w_out, ln1_g, ln1_b,
           pk_w_q, pk_sub_keys, pk_u, pk_v, ln2_g, ln2_b, ple_w_gate, ple_w_proj):
    batch, seq, d = x.shape
    assert d == D_MODEL and w_in.shape[0] == DEPTH and lb_param.shape[0] == DEPTH + 1
    assert seq % HGRN_TS == 0 and seq % MIX_TM == 0
    t = batch * seq
    x2d = x.reshape(t, d)

    w_bf = w_in[0].astype(BF16)
    o_g = _hgrn(x2d, w_bf, lb_param, hg_norm_g, batch, seq)
    x1, x1b = _mix(o_g, x2d, w_bf, conv_w[0], conv_b, w_branch_a[0].astype(BF16),
                   w_branch_b[0].astype(BF16), w_out[0].astype(BF16), ln1_g, ln1_b, seq)

    wq_t = pk_w_q[0].T.astype(BF16)
    keys = pk_sub_keys[0].reshape(PK_HEADS * 2, N_KEYS, N_KEYS).astype(BF16)
    r2, e2, e1, c1 = _topk(x1b, wq_t, keys)
    peer_t = _dense(x1b, pk_u[0].astype(BF16), pk_v[0].T.astype(BF16), r2, e2, e1, c1)

    out = _final(peer_t, x1, p[0].reshape(t, PLE_DIM), ln2_g, ln2_b,
                 ple_w_gate[0].astype(BF16), ple_w_proj[0].astype(BF16))
    return out.reshape(batch, seq, d)
```

```python
import functools
import math

import jax
import jax.numpy as jnp
from jax import lax
from jax.experimental import pallas as pl
from jax.experimental.pallas import tpu as pltpu

D_MODEL = 1024
HEADS = 8
HEAD_DIM = 128
CHUNK = 64
LANES = 128
BF16_ROWS = 16
N_KEYS = 128
PK_HEADS = 8
PK_TOPK = 16
N_EXPERTS = N_KEYS * N_KEYS
PLE_DIM = 256
LN_EPS = 1e-5
RMS_EPS = 1e-6
DEPTH = 1
ALPHA = (2.0 * DEPTH) ** 0.25
VMEM_LIMIT_BYTES = 56 * 1024 * 1024

F32 = jnp.float32
BF16 = jnp.bfloat16
NT_DIMS = (((1,), (1,)), ((), ()))
TN_DIMS = (((0,), (0,)), ((), ()))

HGRN_TS, HGRN_SUB = 512, 256
HGRN_HP = 4
MIX_TM = 256
TOPK_TM = 256
DENSE_TM, DENSE_TE = 256, 4096
DENSE_TH = DENSE_TE // 2
DENSE_TQ = 1024
FINAL_TM = 256


def _lane_blocks(a):
    r, c = a.shape
    return a.reshape(r, c // LANES, LANES).transpose(1, 0, 2)


def _rows_of(ref, rows=slice(None), blocks=None):
    blocks = range(ref.shape[0]) if blocks is None else blocks
    return jnp.concatenate([ref[c, rows, :] for c in blocks], axis=1)


def _params(*sem):
    return pltpu.CompilerParams(dimension_semantics=sem, vmem_limit_bytes=VMEM_LIMIT_BYTES)


def _layer_norm(y, g, b):
    mu = jnp.mean(y, axis=-1, keepdims=True)
    yc = y - mu
    var = jnp.mean(yc * yc, axis=-1, keepdims=True)
    return yc * lax.rsqrt(var + LN_EPS) * g + b


def _hgrn_kernel(x_ref, wq_ref, wf_ref, wi_ref, wg_ref, lbp_ref, g_ref, o_ref, st_ref):
    @pl.when(pl.program_id(2) == 0)
    def _():
        st_ref[...] = jnp.zeros_like(st_ref)

    x = x_ref[...].astype(BF16)
    zq = jnp.dot(x, wq_ref[...], preferred_element_type=F32)
    zf = jnp.dot(x, wf_ref[...], preferred_element_type=F32)
    zi = jnp.dot(x, wi_ref[...], preferred_element_type=F32)
    zg = jnp.dot(x, wg_ref[...], preferred_element_type=F32)

    r = lax.broadcasted_iota(jnp.int32, (HGRN_SUB, HGRN_SUB), 0)
    c = lax.broadcasted_iota(jnp.int32, (HGRN_SUB, HGRN_SUB), 1)
    causal = ((r // CHUNK) == (c // CHUNK)) & (c <= r)
    tril = causal.astype(BF16)
    scale = HEAD_DIM ** -0.5

    n_sub = HGRN_TS // HGRN_SUB
    n_chunk = HGRN_SUB // CHUNK
    items = [(hp, s) for s in range(n_sub) for hp in range(HGRN_HP)]
    k_all, bcum_all, qe_all, v_all, o_all = {}, {}, {}, {}, {}
    for hp, s in items:
        cols = slice(hp * LANES, (hp + 1) * LANES)
        rows = slice(s * HGRN_SUB, (s + 1) * HGRN_SUB)
        lbp = lbp_ref[:, cols]
        lbe = jnp.exp(lbp - jnp.max(lbp, axis=0, keepdims=True))
        lb = lbe[0:1] / jnp.sum(lbe, axis=0, keepdims=True)
        fz = zf[rows, cols]
        logf = jnp.log(lb + (1.0 - lb) * jax.nn.sigmoid(fz))
        k_all[hp, s] = (1.0 - lb) * jax.nn.sigmoid(-fz)
        p0 = logf.astype(BF16)
        p1 = (logf - p0.astype(F32)).astype(BF16)
        bcum_all[hp, s] = (jnp.dot(tril, p0, preferred_element_type=F32)
                           + jnp.dot(tril, p1, preferred_element_type=F32))
    for hp, s in items:
        cols = slice(hp * LANES, (hp + 1) * LANES)
        rows = slice(s * HGRN_SUB, (s + 1) * HGRN_SUB)
        bcum = bcum_all[hp, s]
        qe = (zq[rows, cols] * scale * jnp.exp(bcum)).astype(BF16)
        ke = (k_all[hp, s] * jnp.exp(-bcum)).astype(BF16)
        v_bf = zi[rows, cols].astype(BF16)
        a = lax.dot_general(qe, ke, NT_DIMS, preferred_element_type=F32)
        a = jnp.where(causal, a, 0.0).astype(BF16)
        o_all[hp, s] = jnp.dot(a, v_bf, preferred_element_type=F32)
        qe_all[hp, s], v_all[hp, s] = qe, v_bf
    outs = {item: [] for item in items}
    for s in range(n_sub):
        for ci in range(n_chunk):
            lo, hi = ci * CHUNK, (ci + 1) * CHUNK
            for hp in range(HGRN_HP):
                bcum, k, qe, v_bf = bcum_all[hp, s], k_all[hp, s], qe_all[hp, s], v_all[hp, s]
                b_last = bcum[hi - 1:hi, :]
                k_end = (k[lo:hi] * jnp.exp(b_last - bcum[lo:hi])).astype(BF16)
                st = st_ref[hp]
                o_inter = lax.dot_general(qe[lo:hi], st.astype(BF16), NT_DIMS, preferred_element_type=F32)
                ds_t = lax.dot_general(v_bf[lo:hi], k_end, TN_DIMS, preferred_element_type=F32)
                st_ref[hp] = st * jnp.exp(b_last) + ds_t
                outs[hp, s].append(o_all[hp, s][lo:hi] + o_inter)
    for hp, s in items:
        cols = slice(hp * LANES, (hp + 1) * LANES)
        rows = slice(s * HGRN_SUB, (s + 1) * HGRN_SUB)
        o = jnp.concatenate(outs[hp, s], axis=0)
        o = o * lax.rsqrt(jnp.mean(o * o, axis=-1, keepdims=True) + RMS_EPS) * g_ref[:, cols]
        og = zg[rows, cols]
        o_ref[rows, cols] = (o * (og * jax.nn.sigmoid(og))).astype(BF16)


def _hgrn(x2d, w_bf, lb_param, norm_g, batch, seq):
    nt = seq // HGRN_TS
    nhb = HEADS // HGRN_HP
    width = HGRN_HP * LANES

    def w_cols(group):
        return pl.BlockSpec((D_MODEL, width), lambda b, hb, n: (0, group * nhb + hb))

    return pl.pallas_call(
        _hgrn_kernel,
        grid=(batch, nhb, nt),
        in_specs=[pl.BlockSpec((HGRN_TS, D_MODEL), lambda b, hb, n: (b * nt + n, 0)),
                  w_cols(0), w_cols(1), w_cols(2), w_cols(3),
                  pl.BlockSpec((DEPTH + 1, width), lambda b, hb, n: (0, hb)),
                  pl.BlockSpec((1, width), lambda b, hb, n: (0, hb))],
        out_specs=pl.BlockSpec((HGRN_TS, width), lambda b, hb, n: (b * nt + n, hb)),
        out_shape=jax.ShapeDtypeStruct((batch * seq, D_MODEL), BF16),
        scratch_shapes=[pltpu.VMEM((HGRN_HP, HEAD_DIM, HEAD_DIM), F32)],
        compiler_params=_params("parallel", "parallel", "arbitrary"),
        name="hgrn",
    )(x2d, w_bf, w_bf, w_bf, w_bf, lb_param, norm_g)


def _mix_kernel(o_ref, x_ref, wcb_ref, wcc_ref, wch_ref, wga_ref, wgb_ref, cw_ref, cbias_ref, wa_ref, wb_ref, wo_ref,
                g_ref, b_ref, x1_ref, x1b_ref, halo_ref, *, tiles_per_seq):
    @pl.when((pl.program_id(0) % tiles_per_seq) == 0)
    def _():
        halo_ref[...] = jnp.zeros_like(halo_ref)

    x = x_ref[...]
    xb = x.astype(BF16)
    cb, cc, ch, ga, gb = (jnp.dot(xb, w_ref[...], preferred_element_type=F32)
                          for w_ref in (wcb_ref, wcc_ref, wch_ref, wga_ref, wgb_ref))
    u = cc * ch
    halo = halo_ref[...]
    halo_ref[...] = u[MIX_TM - 8:, :]
    h1 = halo[7:8, :]
    h2 = halo[6:7, :]
    row = lax.broadcasted_iota(jnp.int32, (MIX_TM, D_MODEL), 0)
    u1 = jnp.where(row == 0, h1, pltpu.roll(u, 1, 0))
    u2 = jnp.where(row == 0, h2, jnp.where(row == 1, h1, pltpu.roll(u, 2, 0)))
    y = cbias_ref[...] + cw_ref[0:1, :] * u2 + cw_ref[1:2, :] * u1 + cw_ref[2:3, :] * u
    yc = (cb * y).astype(BF16)
    ta = jnp.dot(o_ref[...], wa_ref[...], preferred_element_type=F32)
    tb = jnp.dot(yc, wb_ref[...], preferred_element_type=F32)
    merged = jax.nn.sigmoid(ga) * ta + jax.nn.sigmoid(gb) * tb
    mix = jnp.dot(merged.astype(BF16), wo_ref[...], preferred_element_type=F32)
    x1 = _layer_norm(ALPHA * x + mix, g_ref[...], b_ref[...])
    x1_ref[...] = x1
    x1b = x1.astype(BF16)
    for c in range(D_MODEL // LANES):
        x1b_ref[c] = x1b[:, c * LANES:(c + 1) * LANES]


def _mix(o_g, x2d, w_bf, conv_w, conv_b, wa, wb, wo, ln_g, ln_b, seq):
    t = x2d.shape[0]

    def full(shape):
        return pl.BlockSpec(shape, lambda i: (0,) * len(shape), pipeline_mode=pl.Buffered(1))

    def w_cols(group):
        return pl.BlockSpec((D_MODEL, D_MODEL), lambda i: (0, group), pipeline_mode=pl.Buffered(1))

    row_blk = pl.BlockSpec((MIX_TM, D_MODEL), lambda i: (i, 0))
    return pl.pallas_call(
        functools.partial(_mix_kernel, tiles_per_seq=seq // MIX_TM),
        grid=(t // MIX_TM,),
        in_specs=[row_blk, row_blk, w_cols(4), w_cols(5), w_cols(6), w_cols(7), w_cols(8),
                  full((3, D_MODEL)), full((1, D_MODEL)),
                  full((D_MODEL, D_MODEL)), full((D_MODEL, D_MODEL)), full((D_MODEL, D_MODEL)),
                  full((1, D_MODEL)), full((1, D_MODEL))],
        out_specs=[row_blk, pl.BlockSpec((D_MODEL // LANES, MIX_TM, LANES), lambda i: (0, i, 0))],
        out_shape=[jax.ShapeDtypeStruct((t, D_MODEL), F32), jax.ShapeDtypeStruct((D_MODEL // LANES, t, LANES), BF16)],
        scratch_shapes=[pltpu.VMEM((8, D_MODEL), F32)],
        compiler_params=_params("arbitrary"),
        name="mix",
    )(o_g, x2d, w_bf, w_bf, w_bf, w_bf, w_bf, conv_w, conv_b, wa, wb, wo, ln_g, ln_b)


def _kth_largest(s, k):
    for _ in range(k - 1):
        s = jnp.where(s == jnp.max(s, axis=0, keepdims=True), -jnp.inf, s)
    return jnp.max(s, axis=0, keepdims=True)


def _oddeven_merge(lo, hi, r):
    step = r * 2
    if step < hi - lo:
        yield from _oddeven_merge(lo, hi, step)
        yield from _oddeven_merge(lo + r, hi, step)
        yield from [(i, i + r) for i in range(lo + r, hi - r, step)]
    else:
        yield (lo, lo + r)


def _oddeven_merge_sort(lo, hi):
    if hi - lo >= 1:
        mid = lo + (hi - lo) // 2
        yield from _oddeven_merge_sort(lo, mid)
        yield from _oddeven_merge_sort(mid + 1, hi)
        yield from _oddeven_merge(lo, hi, 1)


def _exchange(v, i, j):
    v[i], v[j] = jnp.maximum(v[i], v[j]), jnp.minimum(v[i], v[j])


def _top16_sorted(s):
    n = PK_TOPK
    v = [s[k * 8:(k + 1) * 8, :] for k in range(n)]
    for i, j in _oddeven_merge_sort(0, n - 1):
        _exchange(v, i, j)
    for shift in (4, 2, 1):
        rolled = [pltpu.roll(vk, shift, 0) for vk in v]
        v = [jnp.maximum(v[k], rolled[n - 1 - k]) for k in range(n)]
        d = n // 2
        while d:
            for k in range(n):
                if not k & d:
                    _exchange(v, k, k + d)
            d //= 2
    return v


def _count_true(test, b):
    assert len(b) == 16

    def pick(t, v):
        return jnp.where(t, v, 0.0)

    t8 = test(b[7])
    t4 = test(jnp.where(t8, b[11], b[3]))
    t2 = test(jnp.where(t8, jnp.where(t4, b[13], b[9]), jnp.where(t4, b[5], b[1])))
    upper = jnp.where(t4, jnp.where(t2, b[14], b[12]), jnp.where(t2, b[10], b[8]))
    lower = jnp.where(t4, jnp.where(t2, b[6], b[4]), jnp.where(t2, b[2], b[0]))
    t1 = test(jnp.where(t8, upper, lower))
    t16 = test(b[15])
    return pick(t8, 8.0) + pick(t4, 4.0) + pick(t2, 2.0) + pick(t1, 1.0) + pick(t16, 1.0)


def _gather_sublanes(reps):
    sub = lax.broadcasted_iota(jnp.int32, reps[0].shape, 0)
    out = reps[0]
    for r in range(1, len(reps)):
        out = jnp.where(sub == r, reps[r], out)
    return out


def _topk_kernel(x_ref, wq_ref, keys_ref, r2_ref, e2_ref, e1_ref, c1_ref):
    q_t = lax.dot_general(wq_ref[...], _rows_of(x_ref), NT_DIMS, preferred_element_type=F32)
    for h in range(PK_HEADS):
        q1 = q_t[(2 * h) * LANES:(2 * h + 1) * LANES].astype(BF16)
        q2 = q_t[(2 * h + 1) * LANES:(2 * h + 2) * LANES].astype(BF16)
        s1 = jnp.dot(keys_ref[2 * h], q1, preferred_element_type=F32)
        s2 = jnp.dot(keys_ref[2 * h + 1], q2, preferred_element_type=F32)
        a_rep = _top16_sorted(s1)
        b_rep = _top16_sorted(s2)
        a = [v[0:1, :] for v in a_rep]
        b = [v[0:1, :] for v in b_rep]
        half = PK_TOPK // 2
        b_lo, b_hi = _gather_sublanes(b_rep[:half]), _gather_sublanes(b_rep[half:])
        a_hi = _gather_sublanes(a_rep[half:])
        cand = jnp.concatenate([a_rep[0] + b_lo, a_rep[0] + b_hi] + [a_rep[r1] + b_lo for r1 in range(1, half)]
                               + [a_hi + b_rep[0]], axis=0)
        tau = _kth_largest(cand, PK_TOPK)
        top = a[0] + b[0]
        z = jnp.sum(jnp.where(cand >= tau, jnp.exp(cand - top), 0.0), axis=0, keepdims=True)
        count1 = _count_true(lambda b_row: (s1 + b_row) >= tau, b)
        rank2 = _count_true(lambda b_row: b_row > s2, b)
        r2_ref[h * N_KEYS:(h + 1) * N_KEYS, :] = rank2.astype(BF16)
        e2_ref[h * N_KEYS:(h + 1) * N_KEYS, :] = jnp.exp(s2 - b[0]).astype(BF16)
        e1_ref[h] = jnp.exp(s1 - a[0]) * (0.5 / z)
        c1_ref[h] = count1


def _topk(x1b, wq_t, keys):
    t = x1b.shape[1]
    side = pl.BlockSpec((PK_HEADS, N_KEYS, TOPK_TM), lambda i: (0, 0, i))
    f32_side = jax.ShapeDtypeStruct((PK_HEADS, N_KEYS, t), F32)
    side2 = pl.BlockSpec((PK_HEADS * N_KEYS, TOPK_TM), lambda i: (0, i))
    bf16_side = jax.ShapeDtypeStruct((PK_HEADS * N_KEYS, t), BF16)
    return pl.pallas_call(
        _topk_kernel,
        grid=(t // TOPK_TM,),
        in_specs=[pl.BlockSpec((D_MODEL // LANES, TOPK_TM, LANES), lambda i: (0, i, 0)),
                  pl.BlockSpec(wq_t.shape, lambda i: (0, 0)),
                  pl.BlockSpec(keys.shape, lambda i: (0, 0, 0))],
        out_specs=[side2, side2, side, side],
        out_shape=[bf16_side, bf16_side, f32_side, f32_side],
        compiler_params=_params("parallel"),
        name="topk",
    )(x1b, wq_t, keys)


def _bcast_rows_bf16(row):
    tile = jnp.broadcast_to(row, (BF16_ROWS, LANES)).astype(BF16)
    return jnp.broadcast_to(tile[None], (N_KEYS // BF16_ROWS, BF16_ROWS, LANES)).reshape(N_KEYS, LANES)


def _expert_act(u, x):
    h_t = lax.dot_general(u, x, NT_DIMS, preferred_element_type=F32)
    return (h_t * (1.0 + lax.erf(h_t * (1.0 / math.sqrt(2.0))))).astype(BF16)


def _act_first_kernel(x_ref, u_ref, act_ref):
    act_ref[...] = _expert_act(_rows_of(u_ref), _rows_of(x_ref))


def _act_first(x1b, u_bf):
    return pl.pallas_call(
        _act_first_kernel,
        grid=(1,),
        in_specs=[pl.BlockSpec((D_MODEL // LANES, DENSE_TM, LANES), lambda i: (0, 0, 0)),
                  pl.BlockSpec((D_MODEL // LANES, DENSE_TH, LANES), lambda i: (0, 0, 0))],
        out_specs=pl.BlockSpec((DENSE_TH, DENSE_TM), lambda i: (0, 0)),
        out_shape=jax.ShapeDtypeStruct((DENSE_TH, DENSE_TM), BF16),
        compiler_params=_params("arbitrary"),
        name="act_first",
    )(x1b, u_bf)


def _dense_kernel(xc_ref, xn_ref, u2_ref, un_ref, vt_ref, r2_ref, e2_ref, e1_ref, c1_ref, act0_ref, o_ref,
                  act_a, act_b):
    i, j = pl.program_id(0), pl.program_id(1)

    @pl.when((i == 0) & (j == 0))
    def _():
        act_a[...] = act0_ref[...]

    @pl.when(j == 0)
    def _():
        o_ref[...] = jnp.zeros_like(o_ref)

    n_key1 = DENSE_TQ // N_KEYS
    n_qq = DENSE_TH // DENSE_TQ
    acc = None
    for half, (src, dst, u_nx, x_nx) in enumerate(((act_a, act_b, u2_ref, xc_ref), (act_b, act_a, un_ref, xn_ref))):
        for qq in range(n_qq):
            q = half * n_qq + qq
            key1_0 = j * (DENSE_TE // N_KEYS) + q * n_key1
            e1_rows = [[e1_ref[h, pl.ds(key1_0 + a, 1), :] for h in range(PK_HEADS)] for a in range(n_key1)]
            c1_rows = [[c1_ref[h, pl.ds(key1_0 + a, 1), :] for h in range(PK_HEADS)] for a in range(n_key1)]
            cols = []
            for tc in range(DENSE_TM // LANES):
                lanes = slice(tc * LANES, (tc + 1) * LANES)
                parts = []
                for a in range(n_key1):
                    w = jnp.zeros((N_KEYS, LANES), BF16)
                    for h in range(PK_HEADS):
                        e1 = _bcast_rows_bf16(e1_rows[a][h][:, lanes])
                        c1 = _bcast_rows_bf16(c1_rows[a][h][:, lanes])
                        keys2 = slice(h * N_KEYS, (h + 1) * N_KEYS)
                        w = w + jnp.where(c1 > r2_ref[keys2, lanes], e1 * e2_ref[keys2, lanes], 0.0)
                    row0 = qq * DENSE_TQ + a * N_KEYS
                    parts.append(w * src[row0:row0 + N_KEYS, lanes])
                cols.append(jnp.concatenate(parts, axis=0))
            a_q = jnp.concatenate(cols, axis=1)
            vt_q = _rows_of(vt_ref, blocks=range(q * (DENSE_TQ // LANES), (q + 1) * (DENSE_TQ // LANES)))
            d = jnp.dot(vt_q, a_q, preferred_element_type=F32)
            acc = d if acc is None else acc + d
            rows = slice(qq * DENSE_TQ, (qq + 1) * DENSE_TQ)
            dst[rows, :] = _expert_act(_rows_of(u_nx, rows), _rows_of(x_nx))
    o_ref[...] += acc


def _dense(x1b, u_bf, vt_bf, r2, e2, e1, c1):
    t = x1b.shape[1]
    n_blk = D_MODEL // LANES
    n_i, n_j = t // DENSE_TM, N_EXPERTS // DENSE_TE
    side = pl.BlockSpec((PK_HEADS, N_KEYS, DENSE_TM), lambda i, j: (0, 0, i))
    side2 = pl.BlockSpec((PK_HEADS * N_KEYS, DENSE_TM), lambda i, j: (0, i))
    act0 = _act_first(x1b, u_bf)
    return pl.pallas_call(
        _dense_kernel,
        grid=(n_i, n_j),
        in_specs=[pl.BlockSpec((n_blk, DENSE_TM, LANES), lambda i, j: (0, i, 0)),
                  pl.BlockSpec((n_blk, DENSE_TM, LANES),
                               lambda i, j: (0, jnp.minimum(i + (j + 1) // n_j, n_i - 1), 0)),
                  pl.BlockSpec((n_blk, DENSE_TH, LANES), lambda i, j: (0, 2 * j + 1, 0)),
                  pl.BlockSpec((n_blk, DENSE_TH, LANES), lambda i, j: (0, 2 * ((j + 1) % n_j), 0)),
                  pl.BlockSpec((DENSE_TE // LANES, D_MODEL, LANES), lambda i, j: (j, 0, 0)),
                  side2, side2, side, side,
                  pl.BlockSpec((DENSE_TH, DENSE_TM), lambda i, j: (0, 0), pipeline_mode=pl.Buffered(1))],
        out_specs=pl.BlockSpec((D_MODEL, DENSE_TM), lambda i, j: (0, i)),
        out_shape=jax.ShapeDtypeStruct((D_MODEL, t), F32),
        scratch_shapes=[pltpu.VMEM((DENSE_TH, DENSE_TM), BF16), pltpu.VMEM((DENSE_TH, DENSE_TM), BF16)],
        compiler_params=_params("arbitrary", "arbitrary"),
        name="dense",
    )(x1b, x1b, u_bf, u_bf, vt_bf, r2, e2, e1, c1, act0)


def _final_kernel(pt_ref, x1_ref, p_ref, g_ref, b_ref, wg_ref, wp_ref, out_ref):
    peer = pt_ref[...].T
    x2 = _layer_norm(ALPHA * x1_ref[...] + peer, g_ref[...], b_ref[...])
    gate = jax.nn.sigmoid(jnp.dot(x2.astype(BF16), wg_ref[...], preferred_element_type=F32))
    proj = jnp.dot(p_ref[...].astype(BF16), wp_ref[...], preferred_element_type=F32)
    out_ref[...] = x2 + gate * proj


def _final(peer_t, x1, p2d, ln_g, ln_b, wg, wp):
    t = x1.shape[0]

    def full(shape):
        return pl.BlockSpec(shape, lambda i: (0,) * len(shape))

    row_blk = pl.BlockSpec((FINAL_TM, D_MODEL), lambda i: (i, 0))
    return pl.pallas_call(
        _final_kernel,
        grid=(t // FINAL_TM,),
        in_specs=[pl.BlockSpec((D_MODEL, FINAL_TM), lambda i: (0, i)), row_blk,
                  pl.BlockSpec((FINAL_TM, PLE_DIM), lambda i: (i, 0)),
                  full((1, D_MODEL)), full((1, D_MODEL)),
                  full((D_MODEL, D_MODEL)), full((PLE_DIM, D_MODEL))],
        out_specs=row_blk,
        out_shape=jax.ShapeDtypeStruct((t, D_MODEL), F32),
        compiler_params=_params("parallel"),
        name="final",
    )(peer_t, x1, p2d, ln_g, ln_b, wg, wp)


def kernel(x, p, w_in, lb_param, hg_norm_g, conv_w, conv_b, w_branch_a, w_branch_b, w_out, ln1_g, ln1_b,
           pk_w_q, pk_sub_keys, pk_u, pk_v, ln2_g, ln2_b, ple_w_gate, ple_w_proj):
    batch, seq, d = x.shape
    assert d == D_MODEL and w_in.shape[0] == DEPTH and lb_param.shape[0] == DEPTH + 1
    assert seq % HGRN_TS == 0 and seq % MIX_TM == 0
    t = batch * seq
    x2d = x.reshape(t, d)

    w_bf = w_in[0].astype(BF16)
    o_g = _hgrn(x2d, w_bf, lb_param, hg_norm_g, batch, seq)
    x1, x1b = _mix(o_g, x2d, w_bf, conv_w[0], conv_b, w_branch_a[0].astype(BF16),
                   w_branch_b[0].astype(BF16), w_out[0].astype(BF16), ln1_g, ln1_b, seq)

    wq_t = pk_w_q[0].T.astype(BF16)
    keys = pk_sub_keys[0].reshape(PK_HEADS * 2, N_KEYS, N_KEYS).astype(BF16)
    r2, e2, e1, c1 = _topk(x1b, wq_t, keys)
    u_blk = _lane_blocks(pk_u[0].astype(BF16))
    vt_blk = pk_v[0].astype(BF16).reshape(N_EXPERTS // LANES, LANES, D_MODEL).transpose(0, 2, 1)
    peer_t = _dense(x1b, u_blk, vt_blk, r2, e2, e1, c1)

    out = _final(peer_t, x1, p[0].reshape(t, PLE_DIM), ln2_g, ln2_b,
                 ple_w_gate[0].astype(BF16), ple_w_proj[0].astype(BF16))
    return out.reshape(batch, seq, d)
```

```python
import functools
import math

import jax
import jax.numpy as jnp
from jax import lax
from jax.experimental import pallas as pl
from jax.experimental.pallas import tpu as pltpu

D_MODEL = 1024
HEADS = 8
HEAD_DIM = 128
CHUNK = 64
LANES = 128
BF16_ROWS = 16
N_KEYS = 128
PK_HEADS = 8
PK_TOPK = 16
N_EXPERTS = N_KEYS * N_KEYS
PLE_DIM = 256
LN_EPS = 1e-5
RMS_EPS = 1e-6
DEPTH = 1
ALPHA = (2.0 * DEPTH) ** 0.25
VMEM_LIMIT_BYTES = 56 * 1024 * 1024

F32 = jnp.float32
BF16 = jnp.bfloat16
NT_DIMS = (((1,), (1,)), ((), ()))
TN_DIMS = (((0,), (0,)), ((), ()))

HGRN_TS, HGRN_SUB = 512, 256
HGRN_HP = 8
MIX_TM = 512
TOPK_TM = 256
DENSE_TM, DENSE_TE = 256, 4096
DENSE_TH = DENSE_TE // 2
DENSE_TQ = 1024
FINAL_TM = 256


def _rows_of(ref, rows=slice(None), blocks=None):
    blocks = range(ref.shape[0]) if blocks is None else blocks
    return jnp.concatenate([ref[c, rows, :] for c in blocks], axis=1)


def _params(*sem):
    return pltpu.CompilerParams(dimension_semantics=sem, vmem_limit_bytes=VMEM_LIMIT_BYTES)


def _layer_norm(y, g, b):
    mu = jnp.mean(y, axis=-1, keepdims=True)
    yc = y - mu
    var = jnp.mean(yc * yc, axis=-1, keepdims=True)
    return yc * lax.rsqrt(var + LN_EPS) * g + b


def _hgrn_kernel(x_ref, wq_ref, wf_ref, wi_ref, wg_ref, lbp_ref, g_ref, o_ref, st_ref):
    @pl.when(pl.program_id(2) == 0)
    def _():
        st_ref[...] = jnp.zeros_like(st_ref)

    x = x_ref[...].astype(BF16)
    zq = jnp.dot(x, wq_ref[...], preferred_element_type=F32)
    zf = jnp.dot(x, wf_ref[...], preferred_element_type=F32)
    zi = jnp.dot(x, wi_ref[...], preferred_element_type=F32)
    zg = jnp.dot(x, wg_ref[...], preferred_element_type=F32)

    r = lax.broadcasted_iota(jnp.int32, (HGRN_SUB, HGRN_SUB), 0)
    c = lax.broadcasted_iota(jnp.int32, (HGRN_SUB, HGRN_SUB), 1)
    causal = ((r // CHUNK) == (c // CHUNK)) & (c <= r)
    tril = causal.astype(BF16)
    scale = HEAD_DIM ** -0.5

    n_sub = HGRN_TS // HGRN_SUB
    n_chunk = HGRN_SUB // CHUNK
    items = [(hp, s) for s in range(n_sub) for hp in range(HGRN_HP)]
    k_all, bcum_all, qe_all, v_all, o_all = {}, {}, {}, {}, {}
    for hp, s in items:
        cols = slice(hp * LANES, (hp + 1) * LANES)
        rows = slice(s * HGRN_SUB, (s + 1) * HGRN_SUB)
        lbp = lbp_ref[:, cols]
        lbe = jnp.exp(lbp - jnp.max(lbp, axis=0, keepdims=True))
        lb = lbe[0:1] / jnp.sum(lbe, axis=0, keepdims=True)
        fz = zf[rows, cols]
        logf = jnp.log(lb + (1.0 - lb) * jax.nn.sigmoid(fz))
        k_all[hp, s] = (1.0 - lb) * jax.nn.sigmoid(-fz)
        p0 = logf.astype(BF16)
        p1 = (logf - p0.astype(F32)).astype(BF16)
        bcum_all[hp, s] = (jnp.dot(tril, p0, preferred_element_type=F32)
                           + jnp.dot(tril, p1, preferred_element_type=F32))
    for hp, s in items:
        cols = slice(hp * LANES, (hp + 1) * LANES)
        rows = slice(s * HGRN_SUB, (s + 1) * HGRN_SUB)
        bcum = bcum_all[hp, s]
        qe = (zq[rows, cols] * scale * jnp.exp(bcum)).astype(BF16)
        ke = (k_all[hp, s] * jnp.exp(-bcum)).astype(BF16)
        v_bf = zi[rows, cols].astype(BF16)
        a = lax.dot_general(qe, ke, NT_DIMS, preferred_element_type=F32)
        a = jnp.where(causal, a, 0.0).astype(BF16)
        o_all[hp, s] = jnp.dot(a, v_bf, preferred_element_type=F32)
        qe_all[hp, s], v_all[hp, s] = qe, v_bf
    outs = {item: [] for item in items}
    for s in range(n_sub):
        for ci in range(n_chunk):
            lo, hi = ci * CHUNK, (ci + 1) * CHUNK
            for hp in range(HGRN_HP):
                bcum, k, qe, v_bf = bcum_all[hp, s], k_all[hp, s], qe_all[hp, s], v_all[hp, s]
                b_last = bcum[hi - 1:hi, :]
                k_end = (k[lo:hi] * jnp.exp(b_last - bcum[lo:hi])).astype(BF16)
                st = st_ref[hp]
                o_inter = lax.dot_general(qe[lo:hi], st.astype(BF16), NT_DIMS, preferred_element_type=F32)
                ds_t = lax.dot_general(v_bf[lo:hi], k_end, TN_DIMS, preferred_element_type=F32)
                st_ref[hp] = st * jnp.exp(b_last) + ds_t
                outs[hp, s].append(o_all[hp, s][lo:hi] + o_inter)
    for hp, s in items:
        cols = slice(hp * LANES, (hp + 1) * LANES)
        rows = slice(s * HGRN_SUB, (s + 1) * HGRN_SUB)
        o = jnp.concatenate(outs[hp, s], axis=0)
        o = o * lax.rsqrt(jnp.mean(o * o, axis=-1, keepdims=True) + RMS_EPS) * g_ref[:, cols]
        og = zg[rows, cols]
        o_ref[rows, cols] = (o * (og * jax.nn.sigmoid(og))).astype(BF16)


def _hgrn(x2d, w_bf, lb_param, norm_g, batch, seq):
    nt = seq // HGRN_TS
    nhb = HEADS // HGRN_HP
    width = HGRN_HP * LANES

    def w_cols(group):
        return pl.BlockSpec((D_MODEL, width), lambda b, hb, n: (0, group * nhb + hb))

    return pl.pallas_call(
        _hgrn_kernel,
        grid=(batch, nhb, nt),
        in_specs=[pl.BlockSpec((HGRN_TS, D_MODEL), lambda b, hb, n: (b * nt + n, 0)),
                  w_cols(0), w_cols(1), w_cols(2), w_cols(3),
                  pl.BlockSpec((DEPTH + 1, width), lambda b, hb, n: (0, hb)),
                  pl.BlockSpec((1, width), lambda b, hb, n: (0, hb))],
        out_specs=pl.BlockSpec((HGRN_TS, width), lambda b, hb, n: (b * nt + n, hb)),
        out_shape=jax.ShapeDtypeStruct((batch * seq, D_MODEL), BF16),
        scratch_shapes=[pltpu.VMEM((HGRN_HP, HEAD_DIM, HEAD_DIM), F32)],
        compiler_params=_params("parallel", "parallel", "arbitrary"),
        name="hgrn",
    )(x2d, w_bf, w_bf, w_bf, w_bf, lb_param, norm_g)


def _mix_kernel(o_ref, x_ref, wcb_ref, wcc_ref, wch_ref, wga_ref, wgb_ref, cw_ref, cbias_ref, wa_ref, wb_ref, wo_ref,
                g_ref, b_ref, x1_ref, x1b_ref, halo_ref, *, tiles_per_seq):
    @pl.when((pl.program_id(0) % tiles_per_seq) == 0)
    def _():
        halo_ref[...] = jnp.zeros_like(halo_ref)

    x = x_ref[...]
    xb = x.astype(BF16)
    cb, cc, ch, ga, gb = (jnp.dot(xb, w_ref[...], preferred_element_type=F32)
                          for w_ref in (wcb_ref, wcc_ref, wch_ref, wga_ref, wgb_ref))
    u = cc * ch
    halo = halo_ref[...]
    halo_ref[...] = u[MIX_TM - 8:, :]
    h1 = halo[7:8, :]
    h2 = halo[6:7, :]
    row = lax.broadcasted_iota(jnp.int32, (MIX_TM, D_MODEL), 0)
    u1 = jnp.where(row == 0, h1, pltpu.roll(u, 1, 0))
    u2 = jnp.where(row == 0, h2, jnp.where(row == 1, h1, pltpu.roll(u, 2, 0)))
    y = cbias_ref[...] + cw_ref[0:1, :] * u2 + cw_ref[1:2, :] * u1 + cw_ref[2:3, :] * u
    yc = (cb * y).astype(BF16)
    ta = jnp.dot(o_ref[...], wa_ref[...], preferred_element_type=F32)
    tb = jnp.dot(yc, wb_ref[...], preferred_element_type=F32)
    merged = jax.nn.sigmoid(ga) * ta + jax.nn.sigmoid(gb) * tb
    mix = jnp.dot(merged.astype(BF16), wo_ref[...], preferred_element_type=F32)
    x1 = _layer_norm(ALPHA * x + mix, g_ref[...], b_ref[...])
    x1_ref[...] = x1
    x1b = x1.astype(BF16)
    for c in range(D_MODEL // LANES):
        x1b_ref[c] = x1b[:, c * LANES:(c + 1) * LANES]


def _mix(o_g, x2d, w_bf, conv_w, conv_b, wa, wb, wo, ln_g, ln_b, seq):
    t = x2d.shape[0]

    def full(shape):
        return pl.BlockSpec(shape, lambda i: (0,) * len(shape), pipeline_mode=pl.Buffered(1))

    def w_cols(group):
        return pl.BlockSpec((D_MODEL, D_MODEL), lambda i: (0, group), pipeline_mode=pl.Buffered(1))

    row_blk = pl.BlockSpec((MIX_TM, D_MODEL), lambda i: (i, 0))
    return pl.pallas_call(
        functools.partial(_mix_kernel, tiles_per_seq=seq // MIX_TM),
        grid=(t // MIX_TM,),
        in_specs=[row_blk, row_blk, w_cols(4), w_cols(5), w_cols(6), w_cols(7), w_cols(8),
                  full((3, D_MODEL)), full((1, D_MODEL)),
                  full((D_MODEL, D_MODEL)), full((D_MODEL, D_MODEL)), full((D_MODEL, D_MODEL)),
                  full((1, D_MODEL)), full((1, D_MODEL))],
        out_specs=[row_blk, pl.BlockSpec((D_MODEL // LANES, MIX_TM, LANES), lambda i: (0, i, 0))],
        out_shape=[jax.ShapeDtypeStruct((t, D_MODEL), F32), jax.ShapeDtypeStruct((D_MODEL // LANES, t, LANES), BF16)],
        scratch_shapes=[pltpu.VMEM((8, D_MODEL), F32)],
        compiler_params=_params("arbitrary"),
        name="mix",
    )(o_g, x2d, w_bf, w_bf, w_bf, w_bf, w_bf, conv_w, conv_b, wa, wb, wo, ln_g, ln_b)


def _kth_largest(s, k):
    for _ in range(k - 1):
        s = jnp.where(s == jnp.max(s, axis=0, keepdims=True), -jnp.inf, s)
    return jnp.max(s, axis=0, keepdims=True)


def _oddeven_merge(lo, hi, r):
    step = r * 2
    if step < hi - lo:
        yield from _oddeven_merge(lo, hi, step)
        yield from _oddeven_merge(lo + r, hi, step)
        yield from [(i, i + r) for i in range(lo + r, hi - r, step)]
    else:
        yield (lo, lo + r)


def _oddeven_merge_sort(lo, hi):
    if hi - lo >= 1:
        mid = lo + (hi - lo) // 2
        yield from _oddeven_merge_sort(lo, mid)
        yield from _oddeven_merge_sort(mid + 1, hi)
        yield from _oddeven_merge(lo, hi, 1)


def _exchange(v, i, j):
    v[i], v[j] = jnp.maximum(v[i], v[j]), jnp.minimum(v[i], v[j])


def _top16_sorted(s):
    n = PK_TOPK
    v = [s[k * 8:(k + 1) * 8, :] for k in range(n)]
    for i, j in _oddeven_merge_sort(0, n - 1):
        _exchange(v, i, j)
    for shift in (4, 2, 1):
        rolled = [pltpu.roll(vk, shift, 0) for vk in v]
        v = [jnp.maximum(v[k], rolled[n - 1 - k]) for k in range(n)]
        d = n // 2
        while d:
            for k in range(n):
                if not k & d:
                    _exchange(v, k, k + d)
            d //= 2
    return v


def _count_true(test, b):
    assert len(b) == 16

    def pick(t, v):
        return jnp.where(t, v, 0.0)

    t8 = test(b[7])
    t4 = test(jnp.where(t8, b[11], b[3]))
    t2 = test(jnp.where(t8, jnp.where(t4, b[13], b[9]), jnp.where(t4, b[5], b[1])))
    upper = jnp.where(t4, jnp.where(t2, b[14], b[12]), jnp.where(t2, b[10], b[8]))
    lower = jnp.where(t4, jnp.where(t2, b[6], b[4]), jnp.where(t2, b[2], b[0]))
    t1 = test(jnp.where(t8, upper, lower))
    t16 = test(b[15])
    return pick(t8, 8.0) + pick(t4, 4.0) + pick(t2, 2.0) + pick(t1, 1.0) + pick(t16, 1.0)


def _gather_sublanes(reps):
    sub = lax.broadcasted_iota(jnp.int32, reps[0].shape, 0)
    out = reps[0]
    for r in range(1, len(reps)):
        out = jnp.where(sub == r, reps[r], out)
    return out


def _topk_kernel(x_ref, wq_ref, keys_ref, r2_ref, e2_ref, e1_ref, c1_ref):
    q_t = lax.dot_general(wq_ref[...], _rows_of(x_ref), NT_DIMS, preferred_element_type=F32)
    for h in range(PK_HEADS):
        q1 = q_t[(2 * h) * LANES:(2 * h + 1) * LANES].astype(BF16)
        q2 = q_t[(2 * h + 1) * LANES:(2 * h + 2) * LANES].astype(BF16)
        s1 = jnp.dot(keys_ref[2 * h], q1, preferred_element_type=F32)
        s2 = jnp.dot(keys_ref[2 * h + 1], q2, preferred_element_type=F32)
        a_rep = _top16_sorted(s1)
        b_rep = _top16_sorted(s2)
        a = [v[0:1, :] for v in a_rep]
        b = [v[0:1, :] for v in b_rep]
        half = PK_TOPK // 2
        b_lo, b_hi = _gather_sublanes(b_rep[:half]), _gather_sublanes(b_rep[half:])
        a_hi = _gather_sublanes(a_rep[half:])
        cand = jnp.concatenate([a_rep[0] + b_lo, a_rep[0] + b_hi] + [a_rep[r1] + b_lo for r1 in range(1, half)]
                               + [a_hi + b_rep[0]], axis=0)
        tau = _kth_largest(cand, PK_TOPK)
        top = a[0] + b[0]
        z = jnp.sum(jnp.where(cand >= tau, jnp.exp(cand - top), 0.0), axis=0, keepdims=True)
        count1 = _count_true(lambda b_row: (s1 + b_row) >= tau, b)
        rank2 = _count_true(lambda b_row: b_row > s2, b)
        r2_ref[h * N_KEYS:(h + 1) * N_KEYS, :] = rank2.astype(BF16)
        e2_ref[h * N_KEYS:(h + 1) * N_KEYS, :] = jnp.exp(s2 - b[0]).astype(BF16)
        e1_ref[h] = jnp.exp(s1 - a[0]) * (0.5 / z)
        c1_ref[h] = count1


def _topk(x1b, wq_t, keys):
    t = x1b.shape[1]
    side = pl.BlockSpec((PK_HEADS, N_KEYS, TOPK_TM), lambda i: (0, 0, i))
    f32_side = jax.ShapeDtypeStruct((PK_HEADS, N_KEYS, t), F32)
    side2 = pl.BlockSpec((PK_HEADS * N_KEYS, TOPK_TM), lambda i: (0, i))
    bf16_side = jax.ShapeDtypeStruct((PK_HEADS * N_KEYS, t), BF16)
    return pl.pallas_call(
        _topk_kernel,
        grid=(t // TOPK_TM,),
        in_specs=[pl.BlockSpec((D_MODEL // LANES, TOPK_TM, LANES), lambda i: (0, i, 0)),
                  pl.BlockSpec(wq_t.shape, lambda i: (0, 0)),
                  pl.BlockSpec(keys.shape, lambda i: (0, 0, 0))],
        out_specs=[side2, side2, side, side],
        out_shape=[bf16_side, bf16_side, f32_side, f32_side],
        compiler_params=_params("parallel"),
        name="topk",
    )(x1b, wq_t, keys)


def _bcast_rows_bf16(row):
    tile = jnp.broadcast_to(row, (BF16_ROWS, LANES)).astype(BF16)
    return jnp.broadcast_to(tile[None], (N_KEYS // BF16_ROWS, BF16_ROWS, LANES)).reshape(N_KEYS, LANES)


def _expert_act(u, x):
    h_t = lax.dot_general(u, x, NT_DIMS, preferred_element_type=F32)
    return (h_t * (1.0 + lax.erf(h_t * (1.0 / math.sqrt(2.0))))).astype(BF16)


def _act_first_kernel(x_ref, u_ref, act_ref):
    act_ref[...] = _expert_act(u_ref[...], _rows_of(x_ref))


def _act_first(x1b, u_bf):
    return pl.pallas_call(
        _act_first_kernel,
        grid=(1,),
        in_specs=[pl.BlockSpec((D_MODEL // LANES, DENSE_TM, LANES), lambda i: (0, 0, 0)),
                  pl.BlockSpec((DENSE_TH, D_MODEL), lambda i: (0, 0))],
        out_specs=pl.BlockSpec((DENSE_TH, DENSE_TM), lambda i: (0, 0)),
        out_shape=jax.ShapeDtypeStruct((DENSE_TH, DENSE_TM), BF16),
        compiler_params=_params("arbitrary"),
        name="act_first",
    )(x1b, u_bf)


def _dense_kernel(xc_ref, xn_ref, u2_ref, un_ref, vt_ref, r2_ref, e2_ref, e1_ref, c1_ref, act0_ref, o_ref,
                  act_a, act_b):
    i, j = pl.program_id(0), pl.program_id(1)

    @pl.when((i == 0) & (j == 0))
    def _():
        act_a[...] = act0_ref[...]

    @pl.when(j == 0)
    def _():
        o_ref[...] = jnp.zeros_like(o_ref)

    n_key1 = DENSE_TQ // N_KEYS
    n_qq = DENSE_TH // DENSE_TQ
    acc = None
    for half, (src, dst, u_nx, x_nx) in enumerate(((act_a, act_b, u2_ref, xc_ref), (act_b, act_a, un_ref, xn_ref))):
        for qq in range(n_qq):
            q = half * n_qq + qq
            key1_0 = j * (DENSE_TE // N_KEYS) + q * n_key1
            e1_rows = [[e1_ref[h, pl.ds(key1_0 + a, 1), :] for h in range(PK_HEADS)] for a in range(n_key1)]
            c1_rows = [[c1_ref[h, pl.ds(key1_0 + a, 1), :] for h in range(PK_HEADS)] for a in range(n_key1)]
            cols = []
            for tc in range(DENSE_TM // LANES):
                lanes = slice(tc * LANES, (tc + 1) * LANES)
                parts = []
                for a in range(n_key1):
                    w = jnp.zeros((N_KEYS, LANES), BF16)
                    for h in range(PK_HEADS):
                        e1 = _bcast_rows_bf16(e1_rows[a][h][:, lanes])
                        c1 = _bcast_rows_bf16(c1_rows[a][h][:, lanes])
                        keys2 = slice(h * N_KEYS, (h + 1) * N_KEYS)
                        w = w + jnp.where(c1 > r2_ref[keys2, lanes], e1 * e2_ref[keys2, lanes], 0.0)
                    row0 = qq * DENSE_TQ + a * N_KEYS
                    parts.append(w * src[row0:row0 + N_KEYS, lanes])
                cols.append(jnp.concatenate(parts, axis=0))
            a_q = jnp.concatenate(cols, axis=1)
            vt_q = _rows_of(vt_ref, blocks=range(q * (DENSE_TQ // LANES), (q + 1) * (DENSE_TQ // LANES)))
            d = jnp.dot(vt_q, a_q, preferred_element_type=F32)
            acc = d if acc is None else acc + d
            rows = slice(qq * DENSE_TQ, (qq + 1) * DENSE_TQ)
            dst[rows, :] = _expert_act(u_nx[rows, :], _rows_of(x_nx))
    o_ref[...] += acc


def _dense(x1b, u_bf, vt_bf, r2, e2, e1, c1):
    t = x1b.shape[1]
    n_blk = D_MODEL // LANES
    n_i, n_j = t // DENSE_TM, N_EXPERTS // DENSE_TE
    side = pl.BlockSpec((PK_HEADS, N_KEYS, DENSE_TM), lambda i, j: (0, 0, i))
    side2 = pl.BlockSpec((PK_HEADS * N_KEYS, DENSE_TM), lambda i, j: (0, i))
    act0 = _act_first(x1b, u_bf)
    return pl.pallas_call(
        _dense_kernel,
        grid=(n_i, n_j),
        in_specs=[pl.BlockSpec((n_blk, DENSE_TM, LANES), lambda i, j: (0, i, 0)),
                  pl.BlockSpec((n_blk, DENSE_TM, LANES),
                               lambda i, j: (0, jnp.minimum(i + (j + 1) // n_j, n_i - 1), 0)),
                  pl.BlockSpec((DENSE_TH, D_MODEL), lambda i, j: (2 * j + 1, 0)),
                  pl.BlockSpec((DENSE_TH, D_MODEL), lambda i, j: (2 * ((j + 1) % n_j), 0)),
                  pl.BlockSpec((DENSE_TE // LANES, D_MODEL, LANES), lambda i, j: (j, 0, 0)),
                  side2, side2, side, side,
                  pl.BlockSpec((DENSE_TH, DENSE_TM), lambda i, j: (0, 0), pipeline_mode=pl.Buffered(1))],
        out_specs=pl.BlockSpec((D_MODEL, DENSE_TM), lambda i, j: (0, i)),
        out_shape=jax.ShapeDtypeStruct((D_MODEL, t), F32),
        scratch_shapes=[pltpu.VMEM((DENSE_TH, DENSE_TM), BF16), pltpu.VMEM((DENSE_TH, DENSE_TM), BF16)],
        compiler_params=_params("arbitrary", "arbitrary"),
        name="dense",
    )(x1b, x1b, u_bf, u_bf, vt_bf, r2, e2, e1, c1, act0)


def _final_kernel(pt_ref, x1_ref, p_ref, g_ref, b_ref, wg_ref, wp_ref, out_ref):
    peer = pt_ref[...].T
    x2 = _layer_norm(ALPHA * x1_ref[...] + peer, g_ref[...], b_ref[...])
    gate = jax.nn.sigmoid(jnp.dot(x2.astype(BF16), wg_ref[...], preferred_element_type=F32))
    proj = jnp.dot(p_ref[...].astype(BF16), wp_ref[...], preferred_element_type=F32)
    out_ref[...] = x2 + gate * proj


def _final(peer_t, x1, p2d, ln_g, ln_b, wg, wp):
    t = x1.shape[0]

    def full(shape):
        return pl.BlockSpec(shape, lambda i: (0,) * len(shape))

    row_blk = pl.BlockSpec((FINAL_TM, D_MODEL), lambda i: (i, 0))
    return pl.pallas_call(
        _final_kernel,
        grid=(t // FINAL_TM,),
        in_specs=[pl.BlockSpec((D_MODEL, FINAL_TM), lambda i: (0, i)), row_blk,
                  pl.BlockSpec((FINAL_TM, PLE_DIM), lambda i: (i, 0)),
                  full((1, D_MODEL)), full((1, D_MODEL)),
                  full((D_MODEL, D_MODEL)), full((PLE_DIM, D_MODEL))],
        out_specs=row_blk,
        out_shape=jax.ShapeDtypeStruct((t, D_MODEL), F32),
        compiler_params=_params("parallel"),
        name="final",
    )(peer_t, x1, p2d, ln_g, ln_b, wg, wp)


def kernel(x, p, w_in, lb_param, hg_norm_g, conv_w, conv_b, w_branch_a, w_branch_b, w_out, ln1_g, ln1_b,
           pk_w_q, pk_sub_keys, pk_u, pk_v, ln2_g, ln2_b, ple_w_gate, ple_w_proj):
    batch, seq, d = x.shape
    assert d == D_MODEL and w_in.shape[0] == DEPTH and lb_param.shape[0] == DEPTH + 1
    assert seq % HGRN_TS == 0 and seq % MIX_TM == 0
    t = batch * seq
    x2d = x.reshape(t, d)

    w_bf = w_in[0].astype(BF16)
    o_g = _hgrn(x2d, w_bf, lb_param, hg_norm_g, batch, seq)
    x1, x1b = _mix(o_g, x2d, w_bf, conv_w[0], conv_b, w_branch_a[0].astype(BF16),
                   w_branch_b[0].astype(BF16), w_out[0].astype(BF16), ln1_g, ln1_b, seq)

    wq_t = pk_w_q[0].T.astype(BF16)
    keys = pk_sub_keys[0].reshape(PK_HEADS * 2, N_KEYS, N_KEYS).astype(BF16)
    r2, e2, e1, c1 = _topk(x1b, wq_t, keys)
    vt_blk = pk_v[0].astype(BF16).reshape(N_EXPERTS // LANES, LANES, D_MODEL).transpose(0, 2, 1)
    peer_t = _dense(x1b, pk_u[0].astype(BF16), vt_blk, r2, e2, e1, c1)

    out = _final(peer_t, x1, p[0].reshape(t, PLE_DIM), ln2_g, ln2_b,
                 ple_w_gate[0].astype(BF16), ple_w_proj[0].astype(BF16))
    return out.reshape(batch, seq, d)
```

```python
import functools
import math

import jax
import jax.numpy as jnp
from jax import lax
from jax.experimental import pallas as pl
from jax.experimental.pallas import tpu as pltpu

D_MODEL = 1024
HEADS = 8
HEAD_DIM = 128
CHUNK = 64
LANES = 128
BF16_ROWS = 16
N_KEYS = 128
PK_HEADS = 8
PK_TOPK = 16
N_EXPERTS = N_KEYS * N_KEYS
PLE_DIM = 256
LN_EPS = 1e-5
RMS_EPS = 1e-6
DEPTH = 1
ALPHA = (2.0 * DEPTH) ** 0.25
VMEM_LIMIT_BYTES = 56 * 1024 * 1024

F32 = jnp.float32
BF16 = jnp.bfloat16
NT_DIMS = (((1,), (1,)), ((), ()))
TN_DIMS = (((0,), (0,)), ((), ()))

HGRN_TS, HGRN_SUB = 512, 256
HGRN_HP = 8
MIX_TM = 512
TOPK_TM = 256
DENSE_TM, DENSE_TE = 256, 4096
DENSE_TH = DENSE_TE // 2
DENSE_TQ = 1024
FINAL_TM = 512


def _rows_of(ref, rows=slice(None), blocks=None):
    blocks = range(ref.shape[0]) if blocks is None else blocks
    return jnp.concatenate([ref[c, rows, :] for c in blocks], axis=1)


def _params(*sem):
    return pltpu.CompilerParams(dimension_semantics=sem, vmem_limit_bytes=VMEM_LIMIT_BYTES)


def _layer_norm(y, g, b):
    mu = jnp.mean(y, axis=-1, keepdims=True)
    yc = y - mu
    var = jnp.mean(yc * yc, axis=-1, keepdims=True)
    return yc * lax.rsqrt(var + LN_EPS) * g + b


def _hgrn_kernel(x_ref, wq_ref, wf_ref, wi_ref, wg_ref, lbp_ref, g_ref, o_ref, st_ref):
    @pl.when(pl.program_id(2) == 0)
    def _():
        st_ref[...] = jnp.zeros_like(st_ref)

    x = x_ref[...].astype(BF16)
    zq = jnp.dot(x, wq_ref[...], preferred_element_type=F32)
    zf = jnp.dot(x, wf_ref[...], preferred_element_type=F32)
    zi = jnp.dot(x, wi_ref[...], preferred_element_type=F32)
    zg = jnp.dot(x, wg_ref[...], preferred_element_type=F32)

    r = lax.broadcasted_iota(jnp.int32, (HGRN_SUB, HGRN_SUB), 0)
    c = lax.broadcasted_iota(jnp.int32, (HGRN_SUB, HGRN_SUB), 1)
    causal = ((r // CHUNK) == (c // CHUNK)) & (c <= r)
    tril = causal.astype(BF16)
    scale = HEAD_DIM ** -0.5

    n_sub = HGRN_TS // HGRN_SUB
    n_chunk = HGRN_SUB // CHUNK
    items = [(hp, s) for s in range(n_sub) for hp in range(HGRN_HP)]
    k_all, bcum_all, qe_all, v_all, o_all = {}, {}, {}, {}, {}
    for hp, s in items:
        cols = slice(hp * LANES, (hp + 1) * LANES)
        rows = slice(s * HGRN_SUB, (s + 1) * HGRN_SUB)
        lbp = lbp_ref[:, cols]
        lbe = jnp.exp(lbp - jnp.max(lbp, axis=0, keepdims=True))
        lb = lbe[0:1] / jnp.sum(lbe, axis=0, keepdims=True)
        fz = zf[rows, cols]
        logf = jnp.log(lb + (1.0 - lb) * jax.nn.sigmoid(fz))
        k_all[hp, s] = (1.0 - lb) * jax.nn.sigmoid(-fz)
        p0 = logf.astype(BF16)
        p1 = (logf - p0.astype(F32)).astype(BF16)
        both = jnp.dot(tril, jnp.concatenate([p0, p1], axis=1), preferred_element_type=F32)
        bcum_all[hp, s] = both[:, :LANES] + both[:, LANES:]
    for hp, s in items:
        cols = slice(hp * LANES, (hp + 1) * LANES)
        rows = slice(s * HGRN_SUB, (s + 1) * HGRN_SUB)
        bcum = bcum_all[hp, s]
        qe = (zq[rows, cols] * scale * jnp.exp(bcum)).astype(BF16)
        ke = (k_all[hp, s] * jnp.exp(-bcum)).astype(BF16)
        v_bf = zi[rows, cols].astype(BF16)
        a = lax.dot_general(qe, ke, NT_DIMS, preferred_element_type=F32)
        a = jnp.where(causal, a, 0.0).astype(BF16)
        o_all[hp, s] = jnp.dot(a, v_bf, preferred_element_type=F32)
        qe_all[hp, s], v_all[hp, s] = qe, v_bf
    outs = {item: [] for item in items}
    for s in range(n_sub):
        for ci in range(n_chunk):
            lo, hi = ci * CHUNK, (ci + 1) * CHUNK
            for hp in range(HGRN_HP):
                bcum, k, qe, v_bf = bcum_all[hp, s], k_all[hp, s], qe_all[hp, s], v_all[hp, s]
                b_last = bcum[hi - 1:hi, :]
                k_end = (k[lo:hi] * jnp.exp(b_last - bcum[lo:hi])).astype(BF16)
                st = st_ref[hp]
                o_inter = lax.dot_general(qe[lo:hi], st.astype(BF16), NT_DIMS, preferred_element_type=F32)
                ds_t = lax.dot_general(v_bf[lo:hi], k_end, TN_DIMS, preferred_element_type=F32)
                st_ref[hp] = st * jnp.exp(b_last) + ds_t
                outs[hp, s].append(o_all[hp, s][lo:hi] + o_inter)
    for hp, s in items:
        cols = slice(hp * LANES, (hp + 1) * LANES)
        rows = slice(s * HGRN_SUB, (s + 1) * HGRN_SUB)
        o = jnp.concatenate(outs[hp, s], axis=0)
        o = o * lax.rsqrt(jnp.mean(o * o, axis=-1, keepdims=True) + RMS_EPS) * g_ref[:, cols]
        og = zg[rows, cols]
        o_ref[rows, cols] = (o * (og * jax.nn.sigmoid(og))).astype(BF16)


def _hgrn(x2d, w_bf, lb_param, norm_g, batch, seq):
    nt = seq // HGRN_TS
    nhb = HEADS // HGRN_HP
    width = HGRN_HP * LANES

    def w_cols(group):
        return pl.BlockSpec((D_MODEL, width), lambda b, hb, n: (0, group * nhb + hb))

    return pl.pallas_call(
        _hgrn_kernel,
        grid=(batch, nhb, nt),
        in_specs=[pl.BlockSpec((HGRN_TS, D_MODEL), lambda b, hb, n: (b * nt + n, 0)),
                  w_cols(0), w_cols(1), w_cols(2), w_cols(3),
                  pl.BlockSpec((DEPTH + 1, width), lambda b, hb, n: (0, hb)),
                  pl.BlockSpec((1, width), lambda b, hb, n: (0, hb))],
        out_specs=pl.BlockSpec((HGRN_TS, width), lambda b, hb, n: (b * nt + n, hb)),
        out_shape=jax.ShapeDtypeStruct((batch * seq, D_MODEL), BF16),
        scratch_shapes=[pltpu.VMEM((HGRN_HP, HEAD_DIM, HEAD_DIM), F32)],
        compiler_params=_params("parallel", "parallel", "arbitrary"),
        name="hgrn",
    )(x2d, w_bf, w_bf, w_bf, w_bf, lb_param, norm_g)


def _mix_kernel(o_ref, x_ref, wcb_ref, wcc_ref, wch_ref, wga_ref, wgb_ref, cw_ref, cbias_ref, wa_ref, wb_ref, wo_ref,
                g_ref, b_ref, x1_ref, x1b_ref, halo_ref, *, tiles_per_seq):
    @pl.when((pl.program_id(0) % tiles_per_seq) == 0)
    def _():
        halo_ref[...] = jnp.zeros_like(halo_ref)

    x = x_ref[...]
    xb = x.astype(BF16)
    cb, cc, ch, ga, gb = (jnp.dot(xb, w_ref[...], preferred_element_type=F32)
                          for w_ref in (wcb_ref, wcc_ref, wch_ref, wga_ref, wgb_ref))
    u = cc * ch
    halo = halo_ref[...]
    halo_ref[...] = u[MIX_TM - 8:, :]
    h1 = halo[7:8, :]
    h2 = halo[6:7, :]
    row = lax.broadcasted_iota(jnp.int32, (MIX_TM, D_MODEL), 0)
    u1 = jnp.where(row == 0, h1, pltpu.roll(u, 1, 0))
    u2 = jnp.where(row == 0, h2, jnp.where(row == 1, h1, pltpu.roll(u, 2, 0)))
    y = cbias_ref[...] + cw_ref[0:1, :] * u2 + cw_ref[1:2, :] * u1 + cw_ref[2:3, :] * u
    yc = (cb * y).astype(BF16)
    ta = jnp.dot(o_ref[...], wa_ref[...], preferred_element_type=F32)
    tb = jnp.dot(yc, wb_ref[...], preferred_element_type=F32)
    merged = jax.nn.sigmoid(ga) * ta + jax.nn.sigmoid(gb) * tb
    mix = jnp.dot(merged.astype(BF16), wo_ref[...], preferred_element_type=F32)
    x1 = _layer_norm(ALPHA * x + mix, g_ref[...], b_ref[...])
    x1_ref[...] = x1
    x1b = x1.astype(BF16)
    for c in range(D_MODEL // LANES):
        x1b_ref[c] = x1b[:, c * LANES:(c + 1) * LANES]


def _mix(o_g, x2d, w_bf, conv_w, conv_b, wa, wb, wo, ln_g, ln_b, seq):
    t = x2d.shape[0]

    def full(shape):
        return pl.BlockSpec(shape, lambda i: (0,) * len(shape), pipeline_mode=pl.Buffered(1))

    def w_cols(group):
        return pl.BlockSpec((D_MODEL, D_MODEL), lambda i: (0, group), pipeline_mode=pl.Buffered(1))

    row_blk = pl.BlockSpec((MIX_TM, D_MODEL), lambda i: (i, 0))
    return pl.pallas_call(
        functools.partial(_mix_kernel, tiles_per_seq=seq // MIX_TM),
        grid=(t // MIX_TM,),
        in_specs=[row_blk, row_blk, w_cols(4), w_cols(5), w_cols(6), w_cols(7), w_cols(8),
                  full((3, D_MODEL)), full((1, D_MODEL)),
                  full((D_MODEL, D_MODEL)), full((D_MODEL, D_MODEL)), full((D_MODEL, D_MODEL)),
                  full((1, D_MODEL)), full((1, D_MODEL))],
        out_specs=[row_blk, pl.BlockSpec((D_MODEL // LANES, MIX_TM, LANES), lambda i: (0, i, 0))],
        out_shape=[jax.ShapeDtypeStruct((t, D_MODEL), F32), jax.ShapeDtypeStruct((D_MODEL // LANES, t, LANES), BF16)],
        scratch_shapes=[pltpu.VMEM((8, D_MODEL), F32)],
        compiler_params=_params("arbitrary"),
        name="mix",
    )(o_g, x2d, w_bf, w_bf, w_bf, w_bf, w_bf, conv_w, conv_b, wa, wb, wo, ln_g, ln_b)


def _kth_largest(s, k):
    for _ in range(k - 1):
        s = jnp.where(s == jnp.max(s, axis=0, keepdims=True), -jnp.inf, s)
    return jnp.max(s, axis=0, keepdims=True)


def _oddeven_merge(lo, hi, r):
    step = r * 2
    if step < hi - lo:
        yield from _oddeven_merge(lo, hi, step)
        yield from _oddeven_merge(lo + r, hi, step)
        yield from [(i, i + r) for i in range(lo + r, hi - r, step)]
    else:
        yield (lo, lo + r)


def _oddeven_merge_sort(lo, hi):
    if hi - lo >= 1:
        mid = lo + (hi - lo) // 2
        yield from _oddeven_merge_sort(lo, mid)
        yield from _oddeven_merge_sort(mid + 1, hi)
        yield from _oddeven_merge(lo, hi, 1)


def _exchange(v, i, j):
    v[i], v[j] = jnp.maximum(v[i], v[j]), jnp.minimum(v[i], v[j])


def _top16_sorted(s):
    n = PK_TOPK
    v = [s[k * 8:(k + 1) * 8, :] for k in range(n)]
    for i, j in _oddeven_merge_sort(0, n - 1):
        _exchange(v, i, j)
    for shift in (4, 2, 1):
        rolled = [pltpu.roll(vk, shift, 0) for vk in v]
        v = [jnp.maximum(v[k], rolled[n - 1 - k]) for k in range(n)]
        d = n // 2
        while d:
            for k in range(n):
                if not k & d:
                    _exchange(v, k, k + d)
            d //= 2
    return v


def _count_true(test, b):
    assert len(b) == 16

    def pick(t, v):
        return jnp.where(t, v, 0.0)

    t8 = test(b[7])
    t4 = test(jnp.where(t8, b[11], b[3]))
    t2 = test(jnp.where(t8, jnp.where(t4, b[13], b[9]), jnp.where(t4, b[5], b[1])))
    upper = jnp.where(t4, jnp.where(t2, b[14], b[12]), jnp.where(t2, b[10], b[8]))
    lower = jnp.where(t4, jnp.where(t2, b[6], b[4]), jnp.where(t2, b[2], b[0]))
    t1 = test(jnp.where(t8, upper, lower))
    t16 = test(b[15])
    return pick(t8, 8.0) + pick(t4, 4.0) + pick(t2, 2.0) + pick(t1, 1.0) + pick(t16, 1.0)


def _gather_sublanes(reps):
    sub = lax.broadcasted_iota(jnp.int32, reps[0].shape, 0)
    out = reps[0]
    for r in range(1, len(reps)):
        out = jnp.where(sub == r, reps[r], out)
    return out


def _topk_kernel(x_ref, wq_ref, keys_ref, r2_ref, e2_ref, e1_ref, c1_ref):
    q_t = lax.dot_general(wq_ref[...], _rows_of(x_ref), NT_DIMS, preferred_element_type=F32)
    for h in range(PK_HEADS):
        q1 = q_t[(2 * h) * LANES:(2 * h + 1) * LANES].astype(BF16)
        q2 = q_t[(2 * h + 1) * LANES:(2 * h + 2) * LANES].astype(BF16)
        s1 = jnp.dot(keys_ref[2 * h], q1, preferred_element_type=F32)
        s2 = jnp.dot(keys_ref[2 * h + 1], q2, preferred_element_type=F32)
        a_rep = _top16_sorted(s1)
        b_rep = _top16_sorted(s2)
        a = [v[0:1, :] for v in a_rep]
        b = [v[0:1, :] for v in b_rep]
        half = PK_TOPK // 2
        b_lo, b_hi = _gather_sublanes(b_rep[:half]), _gather_sublanes(b_rep[half:])
        a_hi = _gather_sublanes(a_rep[half:])
        cand = jnp.concatenate([a_rep[0] + b_lo, a_rep[0] + b_hi] + [a_rep[r1] + b_lo for r1 in range(1, half)]
                               + [a_hi + b_rep[0]], axis=0)
        tau = _kth_largest(cand, PK_TOPK)
        top = a[0] + b[0]
        z = jnp.sum(jnp.where(cand >= tau, jnp.exp(cand - top), 0.0), axis=0, keepdims=True)
        count1 = _count_true(lambda b_row: (s1 + b_row) >= tau, b)
        rank2 = _count_true(lambda b_row: b_row > s2, b)
        r2_ref[h * N_KEYS:(h + 1) * N_KEYS, :] = rank2.astype(BF16)
        e2_ref[h * N_KEYS:(h + 1) * N_KEYS, :] = jnp.exp(s2 - b[0]).astype(BF16)
        e1_ref[h] = jnp.exp(s1 - a[0]) * (0.5 / z)
        c1_ref[h] = count1


def _topk(x1b, wq_t, keys):
    t = x1b.shape[1]
    side = pl.BlockSpec((PK_HEADS, N_KEYS, TOPK_TM), lambda i: (0, 0, i))
    f32_side = jax.ShapeDtypeStruct((PK_HEADS, N_KEYS, t), F32)
    side2 = pl.BlockSpec((PK_HEADS * N_KEYS, TOPK_TM), lambda i: (0, i))
    bf16_side = jax.ShapeDtypeStruct((PK_HEADS * N_KEYS, t), BF16)
    return pl.pallas_call(
        _topk_kernel,
        grid=(t // TOPK_TM,),
        in_specs=[pl.BlockSpec((D_MODEL // LANES, TOPK_TM, LANES), lambda i: (0, i, 0)),
                  pl.BlockSpec(wq_t.shape, lambda i: (0, 0)),
                  pl.BlockSpec(keys.shape, lambda i: (0, 0, 0))],
        out_specs=[side2, side2, side, side],
        out_shape=[bf16_side, bf16_side, f32_side, f32_side],
        compiler_params=_params("parallel"),
        name="topk",
    )(x1b, wq_t, keys)


def _bcast_rows_bf16(row):
    tile = jnp.broadcast_to(row, (BF16_ROWS, LANES)).astype(BF16)
    return jnp.broadcast_to(tile[None], (N_KEYS // BF16_ROWS, BF16_ROWS, LANES)).reshape(N_KEYS, LANES)


def _expert_act(u, x):
    h_t = lax.dot_general(u, x, NT_DIMS, preferred_element_type=F32)
    return (h_t * (1.0 + lax.erf(h_t * (1.0 / math.sqrt(2.0))))).astype(BF16)


def _act_first_kernel(x_ref, u_ref, act_ref):
    act_ref[...] = _expert_act(u_ref[...], _rows_of(x_ref))


def _act_first(x1b, u_bf):
    return pl.pallas_call(
        _act_first_kernel,
        grid=(1,),
        in_specs=[pl.BlockSpec((D_MODEL // LANES, DENSE_TM, LANES), lambda i: (0, 0, 0)),
                  pl.BlockSpec((DENSE_TH, D_MODEL), lambda i: (0, 0))],
        out_specs=pl.BlockSpec((DENSE_TH, DENSE_TM), lambda i: (0, 0)),
        out_shape=jax.ShapeDtypeStruct((DENSE_TH, DENSE_TM), BF16),
        compiler_params=_params("arbitrary"),
        name="act_first",
    )(x1b, u_bf)


def _dense_kernel(xc_ref, xn_ref, u2_ref, un_ref, vt_ref, r2_ref, e2_ref, e1_ref, c1_ref, act0_ref, o_ref,
                  act_a, act_b):
    i, j = pl.program_id(0), pl.program_id(1)

    @pl.when((i == 0) & (j == 0))
    def _():
        act_a[...] = act0_ref[...]

    @pl.when(j == 0)
    def _():
        o_ref[...] = jnp.zeros_like(o_ref)

    n_key1 = DENSE_TQ // N_KEYS
    n_qq = DENSE_TH // DENSE_TQ
    acc = None
    for half, (src, dst, u_nx, x_nx) in enumerate(((act_a, act_b, u2_ref, xc_ref), (act_b, act_a, un_ref, xn_ref))):
        for qq in range(n_qq):
            q = half * n_qq + qq
            key1_0 = j * (DENSE_TE // N_KEYS) + q * n_key1
            e1_rows = [[e1_ref[h, pl.ds(key1_0 + a, 1), :] for h in range(PK_HEADS)] for a in range(n_key1)]
            c1_rows = [[c1_ref[h, pl.ds(key1_0 + a, 1), :] for h in range(PK_HEADS)] for a in range(n_key1)]
            cols = []
            for tc in range(DENSE_TM // LANES):
                lanes = slice(tc * LANES, (tc + 1) * LANES)
                parts = []
                for a in range(n_key1):
                    w = jnp.zeros((N_KEYS, LANES), BF16)
                    for h in range(PK_HEADS):
                        e1 = _bcast_rows_bf16(e1_rows[a][h][:, lanes])
                        c1 = _bcast_rows_bf16(c1_rows[a][h][:, lanes])
                        keys2 = slice(h * N_KEYS, (h + 1) * N_KEYS)
                        w = w + jnp.where(c1 > r2_ref[keys2, lanes], e1 * e2_ref[keys2, lanes], 0.0)
                    row0 = qq * DENSE_TQ + a * N_KEYS
                    parts.append(w * src[row0:row0 + N_KEYS, lanes])
                cols.append(jnp.concatenate(parts, axis=0))
            a_q = jnp.concatenate(cols, axis=1)
            vt_q = _rows_of(vt_ref, blocks=range(q * (DENSE_TQ // LANES), (q + 1) * (DENSE_TQ // LANES)))
            d = jnp.dot(vt_q, a_q, preferred_element_type=F32)
            acc = d if acc is None else acc + d
            rows = slice(qq * DENSE_TQ, (qq + 1) * DENSE_TQ)
            dst[rows, :] = _expert_act(u_nx[rows, :], _rows_of(x_nx))
    o_ref[...] += acc


def _dense(x1b, u_bf, vt_bf, r2, e2, e1, c1):
    t = x1b.shape[1]
    n_blk = D_MODEL // LANES
    n_i, n_j = t // DENSE_TM, N_EXPERTS // DENSE_TE
    side = pl.BlockSpec((PK_HEADS, N_KEYS, DENSE_TM), lambda i, j: (0, 0, i))
    side2 = pl.BlockSpec((PK_HEADS * N_KEYS, DENSE_TM), lambda i, j: (0, i))
    act0 = _act_first(x1b, u_bf)
    return pl.pallas_call(
        _dense_kernel,
        grid=(n_i, n_j),
        in_specs=[pl.BlockSpec((n_blk, DENSE_TM, LANES), lambda i, j: (0, i, 0)),
                  pl.BlockSpec((n_blk, DENSE_TM, LANES),
                               lambda i, j: (0, jnp.minimum(i + (j + 1) // n_j, n_i - 1), 0)),
                  pl.BlockSpec((DENSE_TH, D_MODEL), lambda i, j: (2 * j + 1, 0)),
                  pl.BlockSpec((DENSE_TH, D_MODEL), lambda i, j: (2 * ((j + 1) % n_j), 0)),
                  pl.BlockSpec((DENSE_TE // LANES, D_MODEL, LANES), lambda i, j: (j, 0, 0)),
                  side2, side2, side, side,
                  pl.BlockSpec((DENSE_TH, DENSE_TM), lambda i, j: (0, 0), pipeline_mode=pl.Buffered(1))],
        out_specs=pl.BlockSpec((D_MODEL, DENSE_TM), lambda i, j: (0, i)),
        out_shape=jax.ShapeDtypeStruct((D_MODEL, t), F32),
        scratch_shapes=[pltpu.VMEM((DENSE_TH, DENSE_TM), BF16), pltpu.VMEM((DENSE_TH, DENSE_TM), BF16)],
        compiler_params=_params("arbitrary", "arbitrary"),
        name="dense",
    )(x1b, x1b, u_bf, u_bf, vt_bf, r2, e2, e1, c1, act0)


def _final_kernel(pt_ref, x1_ref, p_ref, g_ref, b_ref, wg_ref, wp_ref, out_ref):
    peer = pt_ref[...].T
    x2 = _layer_norm(ALPHA * x1_ref[...] + peer, g_ref[...], b_ref[...])
    gate = jax.nn.sigmoid(jnp.dot(x2.astype(BF16), wg_ref[...], preferred_element_type=F32))
    proj = jnp.dot(p_ref[...].astype(BF16), wp_ref[...], preferred_element_type=F32)
    out_ref[...] = x2 + gate * proj


def _final(peer_t, x1, p2d, ln_g, ln_b, wg, wp):
    t = x1.shape[0]

    def full(shape):
        return pl.BlockSpec(shape, lambda i: (0,) * len(shape))

    row_blk = pl.BlockSpec((FINAL_TM, D_MODEL), lambda i: (i, 0))
    return pl.pallas_call(
        _final_kernel,
        grid=(t // FINAL_TM,),
        in_specs=[pl.BlockSpec((D_MODEL, FINAL_TM), lambda i: (0, i)), row_blk,
                  pl.BlockSpec((FINAL_TM, PLE_DIM), lambda i: (i, 0)),
                  full((1, D_MODEL)), full((1, D_MODEL)),
                  full((D_MODEL, D_MODEL)), full((PLE_DIM, D_MODEL))],
        out_specs=row_blk,
        out_shape=jax.ShapeDtypeStruct((t, D_MODEL), F32),
        compiler_params=_params("parallel"),
        name="final",
    )(peer_t, x1, p2d, ln_g, ln_b, wg, wp)


def kernel(x, p, w_in, lb_param, hg_norm_g, conv_w, conv_b, w_branch_a, w_branch_b, w_out, ln1_g, ln1_b,
           pk_w_q, pk_sub_keys, pk_u, pk_v, ln2_g, ln2_b, ple_w_gate, ple_w_proj):
    batch, seq, d = x.shape
    assert d == D_MODEL and w_in.shape[0] == DEPTH and lb_param.shape[0] == DEPTH + 1
    assert seq % HGRN_TS == 0 and seq % MIX_TM == 0
    t = batch * seq
    x2d = x.reshape(t, d)

    w_bf = w_in[0].astype(BF16)
    o_g = _hgrn(x2d, w_bf, lb_param, hg_norm_g, batch, seq)
    x1, x1b = _mix(o_g, x2d, w_bf, conv_w[0], conv_b, w_branch_a[0].astype(BF16),
                   w_branch_b[0].astype(BF16), w_out[0].astype(BF16), ln1_g, ln1_b, seq)

    wq_t = pk_w_q[0].T.astype(BF16)
    keys = pk_sub_keys[0].reshape(PK_HEADS * 2, N_KEYS, N_KEYS).astype(BF16)
    r2, e2, e1, c1 = _topk(x1b, wq_t, keys)
    vt_blk = pk_v[0].astype(BF16).reshape(N_EXPERTS // LANES, LANES, D_MODEL).transpose(0, 2, 1)
    peer_t = _dense(x1b, pk_u[0].astype(BF16), vt_blk, r2, e2, e1, c1)

    out = _final(peer_t, x1, p[0].reshape(t, PLE_DIM), ln2_g, ln2_b,
                 ple_w_gate[0].astype(BF16), ple_w_proj[0].astype(BF16))
    return out.reshape(batch, seq, d)
```

```python
import functools
import math

import jax
import jax.numpy as jnp
from jax import lax
from jax.experimental import pallas as pl
from jax.experimental.pallas import tpu as pltpu

D_MODEL = 1024
HEADS = 8
HEAD_DIM = 128
CHUNK = 64
LANES = 128
BF16_ROWS = 16
N_KEYS = 128
PK_HEADS = 8
PK_TOPK = 16
N_EXPERTS = N_KEYS * N_KEYS
PLE_DIM = 256
LN_EPS = 1e-5
RMS_EPS = 1e-6
DEPTH = 1
ALPHA = (2.0 * DEPTH) ** 0.25
VMEM_LIMIT_BYTES = 56 * 1024 * 1024
DENSE_VMEM_LIMIT_BYTES = 60 * 1024 * 1024
VT_SLOTS = 3

F32 = jnp.float32
BF16 = jnp.bfloat16
NT_DIMS = (((1,), (1,)), ((), ()))
TN_DIMS = (((0,), (0,)), ((), ()))

HGRN_TS, HGRN_SUB = 512, 256
HGRN_HP = 8
MIX_TM = 512
TOPK_TM = 256
DENSE_TM, DENSE_TE = 256, 4096
DENSE_TH = DENSE_TE // 2
DENSE_TQ = 1024
FINAL_TM = 512


def _rows_of(ref, rows=slice(None), blocks=None):
    blocks = range(ref.shape[0]) if blocks is None else blocks
    return jnp.concatenate([ref[c, rows, :] for c in blocks], axis=1)


def _params(*sem):
    return pltpu.CompilerParams(dimension_semantics=sem, vmem_limit_bytes=VMEM_LIMIT_BYTES)


def _layer_norm(y, g, b):
    mu = jnp.mean(y, axis=-1, keepdims=True)
    yc = y - mu
    var = jnp.mean(yc * yc, axis=-1, keepdims=True)
    return yc * lax.rsqrt(var + LN_EPS) * g + b


def _hgrn_kernel(x_ref, wq_ref, wf_ref, wi_ref, wg_ref, lbp_ref, g_ref, o_ref, st_ref):
    @pl.when(pl.program_id(2) == 0)
    def _():
        st_ref[...] = jnp.zeros_like(st_ref)

    x = x_ref[...].astype(BF16)
    zq = jnp.dot(x, wq_ref[...], preferred_element_type=F32)
    zf = jnp.dot(x, wf_ref[...], preferred_element_type=F32)
    zi = jnp.dot(x, wi_ref[...], preferred_element_type=F32)
    zg = jnp.dot(x, wg_ref[...], preferred_element_type=F32)

    r = lax.broadcasted_iota(jnp.int32, (HGRN_SUB, HGRN_SUB), 0)
    c = lax.broadcasted_iota(jnp.int32, (HGRN_SUB, HGRN_SUB), 1)
    causal = ((r // CHUNK) == (c // CHUNK)) & (c <= r)
    tril = causal.astype(BF16)
    scale = HEAD_DIM ** -0.5

    n_sub = HGRN_TS // HGRN_SUB
    n_chunk = HGRN_SUB // CHUNK
    items = [(hp, s) for s in range(n_sub) for hp in range(HGRN_HP)]
    k_all, bcum_all, qe_all, v_all, o_all = {}, {}, {}, {}, {}
    for hp, s in items:
        cols = slice(hp * LANES, (hp + 1) * LANES)
        rows = slice(s * HGRN_SUB, (s + 1) * HGRN_SUB)
        lbp = lbp_ref[:, cols]
        lbe = jnp.exp(lbp - jnp.max(lbp, axis=0, keepdims=True))
        lb = lbe[0:1] / jnp.sum(lbe, axis=0, keepdims=True)
        fz = zf[rows, cols]
        logf = jnp.log(lb + (1.0 - lb) * jax.nn.sigmoid(fz))
        k_all[hp, s] = (1.0 - lb) * jax.nn.sigmoid(-fz)
        p0 = logf.astype(BF16)
        p1 = (logf - p0.astype(F32)).astype(BF16)
        both = jnp.dot(tril, jnp.concatenate([p0, p1], axis=1), preferred_element_type=F32)
        bcum_all[hp, s] = both[:, :LANES] + both[:, LANES:]
    for hp, s in items:
        cols = slice(hp * LANES, (hp + 1) * LANES)
        rows = slice(s * HGRN_SUB, (s + 1) * HGRN_SUB)
        bcum = bcum_all[hp, s]
        qe = (zq[rows, cols] * scale * jnp.exp(bcum)).astype(BF16)
        ke = (k_all[hp, s] * jnp.exp(-bcum)).astype(BF16)
        v_bf = zi[rows, cols].astype(BF16)
        a = lax.dot_general(qe, ke, NT_DIMS, preferred_element_type=F32)
        a = jnp.where(causal, a, 0.0).astype(BF16)
        o_all[hp, s] = jnp.dot(a, v_bf, preferred_element_type=F32)
        qe_all[hp, s], v_all[hp, s] = qe, v_bf
    outs = {item: [] for item in items}
    for s in range(n_sub):
        for ci in range(n_chunk):
            lo, hi = ci * CHUNK, (ci + 1) * CHUNK
            for hp in range(HGRN_HP):
                bcum, k, qe, v_bf = bcum_all[hp, s], k_all[hp, s], qe_all[hp, s], v_all[hp, s]
                b_last = bcum[hi - 1:hi, :]
                k_end = (k[lo:hi] * jnp.exp(b_last - bcum[lo:hi])).astype(BF16)
                st = st_ref[hp]
                o_inter = lax.dot_general(qe[lo:hi], st.astype(BF16), NT_DIMS, preferred_element_type=F32)
                ds_t = lax.dot_general(v_bf[lo:hi], k_end, TN_DIMS, preferred_element_type=F32)
                st_ref[hp] = st * jnp.exp(b_last) + ds_t
                outs[hp, s].append(o_all[hp, s][lo:hi] + o_inter)
    for hp, s in items:
        cols = slice(hp * LANES, (hp + 1) * LANES)
        rows = slice(s * HGRN_SUB, (s + 1) * HGRN_SUB)
        o = jnp.concatenate(outs[hp, s], axis=0)
        o = o * lax.rsqrt(jnp.mean(o * o, axis=-1, keepdims=True) + RMS_EPS) * g_ref[:, cols]
        og = zg[rows, cols]
        o_ref[rows, cols] = (o * (og * jax.nn.sigmoid(og))).astype(BF16)


def _hgrn(x2d, w_bf, lb_param, norm_g, batch, seq):
    nt = seq // HGRN_TS
    nhb = HEADS // HGRN_HP
    width = HGRN_HP * LANES

    def w_cols(group):
        return pl.BlockSpec((D_MODEL, width), lambda b, hb, n: (0, group * nhb + hb))

    return pl.pallas_call(
        _hgrn_kernel,
        grid=(batch, nhb, nt),
        in_specs=[pl.BlockSpec((HGRN_TS, D_MODEL), lambda b, hb, n: (b * nt + n, 0)),
                  w_cols(0), w_cols(1), w_cols(2), w_cols(3),
                  pl.BlockSpec((DEPTH + 1, width), lambda b, hb, n: (0, hb)),
                  pl.BlockSpec((1, width), lambda b, hb, n: (0, hb))],
        out_specs=pl.BlockSpec((HGRN_TS, width), lambda b, hb, n: (b * nt + n, hb)),
        out_shape=jax.ShapeDtypeStruct((batch * seq, D_MODEL), BF16),
        scratch_shapes=[pltpu.VMEM((HGRN_HP, HEAD_DIM, HEAD_DIM), F32)],
        compiler_params=_params("parallel", "parallel", "arbitrary"),
        name="hgrn",
    )(x2d, w_bf, w_bf, w_bf, w_bf, lb_param, norm_g)


def _mix_kernel(o_ref, x_ref, wcb_ref, wcc_ref, wch_ref, wga_ref, wgb_ref, cw_ref, cbias_ref, wa_ref, wb_ref, wo_ref,
                g_ref, b_ref, x1_ref, x1b_ref, halo_ref, *, tiles_per_seq):
    @pl.when((pl.program_id(0) % tiles_per_seq) == 0)
    def _():
        halo_ref[...] = jnp.zeros_like(halo_ref)

    x = x_ref[...]
    xb = x.astype(BF16)
    cb, cc, ch, ga, gb = (jnp.dot(xb, w_ref[...], preferred_element_type=F32)
                          for w_ref in (wcb_ref, wcc_ref, wch_ref, wga_ref, wgb_ref))
    u = cc * ch
    halo = halo_ref[...]
    halo_ref[...] = u[MIX_TM - 8:, :]
    h1 = halo[7:8, :]
    h2 = halo[6:7, :]
    row = lax.broadcasted_iota(jnp.int32, (MIX_TM, D_MODEL), 0)
    u1 = jnp.where(row == 0, h1, pltpu.roll(u, 1, 0))
    u2 = jnp.where(row == 0, h2, jnp.where(row == 1, h1, pltpu.roll(u, 2, 0)))
    y = cbias_ref[...] + cw_ref[0:1, :] * u2 + cw_ref[1:2, :] * u1 + cw_ref[2:3, :] * u
    yc = (cb * y).astype(BF16)
    ta = jnp.dot(o_ref[...], wa_ref[...], preferred_element_type=F32)
    tb = jnp.dot(yc, wb_ref[...], preferred_element_type=F32)
    merged = jax.nn.sigmoid(ga) * ta + jax.nn.sigmoid(gb) * tb
    mix = jnp.dot(merged.astype(BF16), wo_ref[...], preferred_element_type=F32)
    x1 = _layer_norm(ALPHA * x + mix, g_ref[...], b_ref[...])
    x1_ref[...] = x1
    x1b = x1.astype(BF16)
    for c in range(D_MODEL // LANES):
        x1b_ref[c] = x1b[:, c * LANES:(c + 1) * LANES]


def _mix(o_g, x2d, w_bf, conv_w, conv_b, wa, wb, wo, ln_g, ln_b, seq):
    t = x2d.shape[0]

    def full(shape):
        return pl.BlockSpec(shape, lambda i: (0,) * len(shape), pipeline_mode=pl.Buffered(1))

    def w_cols(group):
        return pl.BlockSpec((D_MODEL, D_MODEL), lambda i: (0, group), pipeline_mode=pl.Buffered(1))

    row_blk = pl.BlockSpec((MIX_TM, D_MODEL), lambda i: (i, 0))
    return pl.pallas_call(
        functools.partial(_mix_kernel, tiles_per_seq=seq // MIX_TM),
        grid=(t // MIX_TM,),
        in_specs=[row_blk, row_blk, w_cols(4), w_cols(5), w_cols(6), w_cols(7), w_cols(8),
                  full((3, D_MODEL)), full((1, D_MODEL)),
                  full((D_MODEL, D_MODEL)), full((D_MODEL, D_MODEL)), full((D_MODEL, D_MODEL)),
                  full((1, D_MODEL)), full((1, D_MODEL))],
        out_specs=[row_blk, pl.BlockSpec((D_MODEL // LANES, MIX_TM, LANES), lambda i: (0, i, 0))],
        out_shape=[jax.ShapeDtypeStruct((t, D_MODEL), F32), jax.ShapeDtypeStruct((D_MODEL // LANES, t, LANES), BF16)],
        scratch_shapes=[pltpu.VMEM((8, D_MODEL), F32)],
        compiler_params=_params("arbitrary"),
        name="mix",
    )(o_g, x2d, w_bf, w_bf, w_bf, w_bf, w_bf, conv_w, conv_b, wa, wb, wo, ln_g, ln_b)


def _kth_largest(s, k):
    for _ in range(k - 1):
        s = jnp.where(s == jnp.max(s, axis=0, keepdims=True), -jnp.inf, s)
    return jnp.max(s, axis=0, keepdims=True)


def _oddeven_merge(lo, hi, r):
    step = r * 2
    if step < hi - lo:
        yield from _oddeven_merge(lo, hi, step)
        yield from _oddeven_merge(lo + r, hi, step)
        yield from [(i, i + r) for i in range(lo + r, hi - r, step)]
    else:
        yield (lo, lo + r)


def _oddeven_merge_sort(lo, hi):
    if hi - lo >= 1:
        mid = lo + (hi - lo) // 2
        yield from _oddeven_merge_sort(lo, mid)
        yield from _oddeven_merge_sort(mid + 1, hi)
        yield from _oddeven_merge(lo, hi, 1)


def _exchange(v, i, j):
    v[i], v[j] = jnp.maximum(v[i], v[j]), jnp.minimum(v[i], v[j])


def _top16_sorted(s):
    n = PK_TOPK
    v = [s[k * 8:(k + 1) * 8, :] for k in range(n)]
    for i, j in _oddeven_merge_sort(0, n - 1):
        _exchange(v, i, j)
    for shift in (4, 2, 1):
        rolled = [pltpu.roll(vk, shift, 0) for vk in v]
        v = [jnp.maximum(v[k], rolled[n - 1 - k]) for k in range(n)]
        d = n // 2
        while d:
            for k in range(n):
                if not k & d:
                    _exchange(v, k, k + d)
            d //= 2
    return v


def _count_true(test, b):
    assert len(b) == 16

    def pick(t, v):
        return jnp.where(t, v, 0.0)

    t8 = test(b[7])
    t4 = test(jnp.where(t8, b[11], b[3]))
    t2 = test(jnp.where(t8, jnp.where(t4, b[13], b[9]), jnp.where(t4, b[5], b[1])))
    upper = jnp.where(t4, jnp.where(t2, b[14], b[12]), jnp.where(t2, b[10], b[8]))
    lower = jnp.where(t4, jnp.where(t2, b[6], b[4]), jnp.where(t2, b[2], b[0]))
    t1 = test(jnp.where(t8, upper, lower))
    t16 = test(b[15])
    return pick(t8, 8.0) + pick(t4, 4.0) + pick(t2, 2.0) + pick(t1, 1.0) + pick(t16, 1.0)


def _gather_sublanes(reps):
    sub = lax.broadcasted_iota(jnp.int32, reps[0].shape, 0)
    out = reps[0]
    for r in range(1, len(reps)):
        out = jnp.where(sub == r, reps[r], out)
    return out


def _topk_kernel(x_ref, wq_ref, keys_ref, r2_ref, e2_ref, e1_ref, c1_ref):
    q_t = lax.dot_general(wq_ref[...], _rows_of(x_ref), NT_DIMS, preferred_element_type=F32)
    for h in range(PK_HEADS):
        q1 = q_t[(2 * h) * LANES:(2 * h + 1) * LANES].astype(BF16)
        q2 = q_t[(2 * h + 1) * LANES:(2 * h + 2) * LANES].astype(BF16)
        s1 = jnp.dot(keys_ref[2 * h], q1, preferred_element_type=F32)
        s2 = jnp.dot(keys_ref[2 * h + 1], q2, preferred_element_type=F32)
        a_rep = _top16_sorted(s1)
        b_rep = _top16_sorted(s2)
        a = [v[0:1, :] for v in a_rep]
        b = [v[0:1, :] for v in b_rep]
        half = PK_TOPK // 2
        b_lo, b_hi = _gather_sublanes(b_rep[:half]), _gather_sublanes(b_rep[half:])
        a_hi = _gather_sublanes(a_rep[half:])
        cand = jnp.concatenate([a_rep[0] + b_lo, a_rep[0] + b_hi] + [a_rep[r1] + b_lo for r1 in range(1, half)]
                               + [a_hi + b_rep[0]], axis=0)
        tau = _kth_largest(cand, PK_TOPK)
        top = a[0] + b[0]
        z = jnp.sum(jnp.where(cand >= tau, jnp.exp(cand - top), 0.0), axis=0, keepdims=True)
        count1 = _count_true(lambda b_row: (s1 + b_row) >= tau, b)
        rank2 = _count_true(lambda b_row: b_row > s2, b)
        r2_ref[h * N_KEYS:(h + 1) * N_KEYS, :] = rank2.astype(BF16)
        e2_ref[h * N_KEYS:(h + 1) * N_KEYS, :] = jnp.exp(s2 - b[0]).astype(BF16)
        e1_ref[h] = jnp.exp(s1 - a[0]) * (0.5 / z)
        c1_ref[h] = count1


def _topk(x1b, wq_t, keys):
    t = x1b.shape[1]
    side = pl.BlockSpec((PK_HEADS, N_KEYS, TOPK_TM), lambda i: (0, 0, i))
    f32_side = jax.ShapeDtypeStruct((PK_HEADS, N_KEYS, t), F32)
    side2 = pl.BlockSpec((PK_HEADS * N_KEYS, TOPK_TM), lambda i: (0, i))
    bf16_side = jax.ShapeDtypeStruct((PK_HEADS * N_KEYS, t), BF16)
    return pl.pallas_call(
        _topk_kernel,
        grid=(t // TOPK_TM,),
        in_specs=[pl.BlockSpec((D_MODEL // LANES, TOPK_TM, LANES), lambda i: (0, i, 0)),
                  pl.BlockSpec(wq_t.shape, lambda i: (0, 0)),
                  pl.BlockSpec(keys.shape, lambda i: (0, 0, 0))],
        out_specs=[side2, side2, side, side],
        out_shape=[bf16_side, bf16_side, f32_side, f32_side],
        compiler_params=_params("parallel"),
        name="topk",
    )(x1b, wq_t, keys)


def _bcast_rows_bf16(row):
    tile = jnp.broadcast_to(row, (BF16_ROWS, LANES)).astype(BF16)
    return jnp.broadcast_to(tile[None], (N_KEYS // BF16_ROWS, BF16_ROWS, LANES)).reshape(N_KEYS, LANES)


def _expert_act(u, x):
    h_t = lax.dot_general(u, x, NT_DIMS, preferred_element_type=F32)
    return (h_t * (1.0 + lax.erf(h_t * (1.0 / math.sqrt(2.0))))).astype(BF16)


def _act_first_kernel(x_ref, u_ref, act_ref):
    act_ref[...] = _expert_act(u_ref[...], _rows_of(x_ref))


def _act_first(x1b, u_bf):
    return pl.pallas_call(
        _act_first_kernel,
        grid=(1,),
        in_specs=[pl.BlockSpec((D_MODEL // LANES, DENSE_TM, LANES), lambda i: (0, 0, 0)),
                  pl.BlockSpec((DENSE_TH, D_MODEL), lambda i: (0, 0))],
        out_specs=pl.BlockSpec((DENSE_TH, DENSE_TM), lambda i: (0, 0)),
        out_shape=jax.ShapeDtypeStruct((DENSE_TH, DENSE_TM), BF16),
        compiler_params=_params("arbitrary"),
        name="act_first",
    )(x1b, u_bf)


def _vt_copy(vt_hbm, vt_buf, sem, step, n_j):
    blocks = DENSE_TE // LANES
    slot = step % VT_SLOTS
    return pltpu.make_async_copy(vt_hbm.at[pl.ds((step % n_j) * blocks, blocks)], vt_buf.at[slot], sem.at[slot])


def _dense_kernel(xc_ref, xn_ref, u2_ref, un_ref, vt_hbm, r2_ref, e2_ref, e1_ref, c1_ref, act0_ref, o_ref,
                  act_a, act_b, vt_buf, vt_sem):
    i, j = pl.program_id(0), pl.program_id(1)
    n_j = pl.num_programs(1)
    step = i * n_j + j
    n_steps = pl.num_programs(0) * n_j

    @pl.when(step == 0)
    def _():
        act_a[...] = act0_ref[...]
        _vt_copy(vt_hbm, vt_buf, vt_sem, 0, n_j).start()
        _vt_copy(vt_hbm, vt_buf, vt_sem, 1, n_j).start()

    @pl.when(step + 2 < n_steps)
    def _():
        _vt_copy(vt_hbm, vt_buf, vt_sem, step + 2, n_j).start()

    _vt_copy(vt_hbm, vt_buf, vt_sem, step, n_j).wait()
    vt_ref = vt_buf.at[step % VT_SLOTS]

    @pl.when(j == 0)
    def _():
        o_ref[...] = jnp.zeros_like(o_ref)

    n_key1 = DENSE_TQ // N_KEYS
    n_qq = DENSE_TH // DENSE_TQ
    acc = None
    for half, (src, dst, u_nx, x_nx) in enumerate(((act_a, act_b, u2_ref, xc_ref), (act_b, act_a, un_ref, xn_ref))):
        for qq in range(n_qq):
            q = half * n_qq + qq
            key1_0 = j * (DENSE_TE // N_KEYS) + q * n_key1
            e1_rows = [[e1_ref[h, pl.ds(key1_0 + a, 1), :] for h in range(PK_HEADS)] for a in range(n_key1)]
            c1_rows = [[c1_ref[h, pl.ds(key1_0 + a, 1), :] for h in range(PK_HEADS)] for a in range(n_key1)]
            cols = []
            for tc in range(DENSE_TM // LANES):
                lanes = slice(tc * LANES, (tc + 1) * LANES)
                parts = []
                for a in range(n_key1):
                    w = jnp.zeros((N_KEYS, LANES), BF16)
                    for h in range(PK_HEADS):
                        e1 = _bcast_rows_bf16(e1_rows[a][h][:, lanes])
                        c1 = _bcast_rows_bf16(c1_rows[a][h][:, lanes])
                        keys2 = slice(h * N_KEYS, (h + 1) * N_KEYS)
                        w = w + jnp.where(c1 > r2_ref[keys2, lanes], e1 * e2_ref[keys2, lanes], 0.0)
                    row0 = qq * DENSE_TQ + a * N_KEYS
                    parts.append(w * src[row0:row0 + N_KEYS, lanes])
                cols.append(jnp.concatenate(parts, axis=0))
            a_q = jnp.concatenate(cols, axis=1)
            vt_q = _rows_of(vt_ref, blocks=range(q * (DENSE_TQ // LANES), (q + 1) * (DENSE_TQ // LANES)))
            d = jnp.dot(vt_q, a_q, preferred_element_type=F32)
            acc = d if acc is None else acc + d
            rows = slice(qq * DENSE_TQ, (qq + 1) * DENSE_TQ)
            dst[rows, :] = _expert_act(u_nx[rows, :], _rows_of(x_nx))
    o_ref[...] += acc


def _dense(x1b, u_bf, vt_bf, r2, e2, e1, c1):
    t = x1b.shape[1]
    n_blk = D_MODEL // LANES
    n_i, n_j = t // DENSE_TM, N_EXPERTS // DENSE_TE
    side = pl.BlockSpec((PK_HEADS, N_KEYS, DENSE_TM), lambda i, j: (0, 0, i))
    side2 = pl.BlockSpec((PK_HEADS * N_KEYS, DENSE_TM), lambda i, j: (0, i))
    act0 = _act_first(x1b, u_bf)
    return pl.pallas_call(
        _dense_kernel,
        grid=(n_i, n_j),
        in_specs=[pl.BlockSpec((n_blk, DENSE_TM, LANES), lambda i, j: (0, i, 0)),
                  pl.BlockSpec((n_blk, DENSE_TM, LANES),
                               lambda i, j: (0, jnp.minimum(i + (j + 1) // n_j, n_i - 1), 0)),
                  pl.BlockSpec((DENSE_TH, D_MODEL), lambda i, j: (2 * j + 1, 0)),
                  pl.BlockSpec((DENSE_TH, D_MODEL), lambda i, j: (2 * ((j + 1) % n_j), 0)),
                  pl.BlockSpec(memory_space=pl.ANY),
                  side2, side2, side, side,
                  pl.BlockSpec((DENSE_TH, DENSE_TM), lambda i, j: (0, 0), pipeline_mode=pl.Buffered(1))],
        out_specs=pl.BlockSpec((D_MODEL, DENSE_TM), lambda i, j: (0, i)),
        out_shape=jax.ShapeDtypeStruct((D_MODEL, t), F32),
        scratch_shapes=[pltpu.VMEM((DENSE_TH, DENSE_TM), BF16), pltpu.VMEM((DENSE_TH, DENSE_TM), BF16),
                        pltpu.VMEM((VT_SLOTS, DENSE_TE // LANES, D_MODEL, LANES), BF16),
                        pltpu.SemaphoreType.DMA((VT_SLOTS,))],
        compiler_params=pltpu.CompilerParams(dimension_semantics=("arbitrary", "arbitrary"),
                                             vmem_limit_bytes=DENSE_VMEM_LIMIT_BYTES),
        name="dense",
    )(x1b, x1b, u_bf, u_bf, vt_bf, r2, e2, e1, c1, act0)


def _final_kernel(pt_ref, x1_ref, p_ref, g_ref, b_ref, wg_ref, wp_ref, out_ref):
    peer = pt_ref[...].T
    x2 = _layer_norm(ALPHA * x1_ref[...] + peer, g_ref[...], b_ref[...])
    gate = jax.nn.sigmoid(jnp.dot(x2.astype(BF16), wg_ref[...], preferred_element_type=F32))
    proj = jnp.dot(p_ref[...].astype(BF16), wp_ref[...], preferred_element_type=F32)
    out_ref[...] = x2 + gate * proj


def _final(peer_t, x1, p2d, ln_g, ln_b, wg, wp):
    t = x1.shape[0]

    def full(shape):
        return pl.BlockSpec(shape, lambda i: (0,) * len(shape))

    row_blk = pl.BlockSpec((FINAL_TM, D_MODEL), lambda i: (i, 0))
    return pl.pallas_call(
        _final_kernel,
        grid=(t // FINAL_TM,),
        in_specs=[pl.BlockSpec((D_MODEL, FINAL_TM), lambda i: (0, i)), row_blk,
                  pl.BlockSpec((FINAL_TM, PLE_DIM), lambda i: (i, 0)),
                  full((1, D_MODEL)), full((1, D_MODEL)),
                  full((D_MODEL, D_MODEL)), full((PLE_DIM, D_MODEL))],
        out_specs=row_blk,
        out_shape=jax.ShapeDtypeStruct((t, D_MODEL), F32),
        compiler_params=_params("parallel"),
        name="final",
    )(peer_t, x1, p2d, ln_g, ln_b, wg, wp)


def kernel(x, p, w_in, lb_param, hg_norm_g, conv_w, conv_b, w_branch_a, w_branch_b, w_out, ln1_g, ln1_b,
           pk_w_q, pk_sub_keys, pk_u, pk_v, ln2_g, ln2_b, ple_w_gate, ple_w_proj):
    batch, seq, d = x.shape
    assert d == D_MODEL and w_in.shape[0] == DEPTH and lb_param.shape[0] == DEPTH + 1
    assert seq % HGRN_TS == 0 and seq % MIX_TM == 0
    t = batch * seq
    x2d = x.reshape(t, d)

    w_bf = w_in[0].astype(BF16)
    o_g = _hgrn(x2d, w_bf, lb_param, hg_norm_g, batch, seq)
    x1, x1b = _mix(o_g, x2d, w_bf, conv_w[0], conv_b, w_branch_a[0].astype(BF16),
                   w_branch_b[0].astype(BF16), w_out[0].astype(BF16), ln1_g, ln1_b, seq)

    wq_t = pk_w_q[0].T.astype(BF16)
    keys = pk_sub_keys[0].reshape(PK_HEADS * 2, N_KEYS, N_KEYS).astype(BF16)
    r2, e2, e1, c1 = _topk(x1b, wq_t, keys)
    vt_blk = pk_v[0].astype(BF16).reshape(N_EXPERTS // LANES, LANES, D_MODEL).transpose(0, 2, 1)
    peer_t = _dense(x1b, pk_u[0].astype(BF16), vt_blk, r2, e2, e1, c1)

    out = _final(peer_t, x1, p[0].reshape(t, PLE_DIM), ln2_g, ln2_b,
                 ple_w_gate[0].astype(BF16), ple_w_proj[0].astype(BF16))
    return out.reshape(batch, seq, d)
```
